```python
import math
import jax, jax.numpy as jnp
from jax import lax
import numpy as np

D_MODEL = 1024
BATCH = 8
SEQ = 4096
DEPTH = 4

CHUNK = 64
N_A_LAYERS = DEPTH // 2
N_B_LAYERS = DEPTH - N_A_LAYERS

SSM_EXPAND = 2
SSM_D_INNER = SSM_EXPAND * D_MODEL
SSM_HEAD_DIM = 64
SSM_HEADS = SSM_D_INNER // SSM_HEAD_DIM
SSM_GROUPS = 4
SSM_STATE = 128
SSM_CONV = 4
SSM_GN = SSM_GROUPS * SSM_STATE
SSM_CONV_DIM = SSM_D_INNER + 2 * SSM_GN
SSM_IN_DIM = SSM_D_INNER + SSM_CONV_DIM + SSM_HEADS

ATTN_HEAD_DIM = 64
ATTN_HEADS = D_MODEL // ATTN_HEAD_DIM
Q_BLOCK = 128

N_EXPERTS = 64
TOP_K = 8
N_EXPERT_GROUPS = 8
TOPK_GROUPS = 4
EXPERT_DIM = D_MODEL // 8
SHARED_DIM = EXPERT_DIM
ROUTED_SCALE = 2.5
EXPERT_BLOCK = 8

DN_ALPHA = (2.0 * DEPTH) ** 0.25
DN_BETA = (8.0 * DEPTH) ** -0.25
LN_EPS = 1e-5
RMS_EPS = 1e-5

kernel_name = "yoco_mamba2_fox_moe_deepnorm_adaln"

F32 = jnp.float32


def layer_norm(x, g, b):
    xf = x.astype(F32)
    mu = jnp.mean(xf, axis=-1, keepdims=True)
    var = jnp.mean(jnp.square(xf - mu), axis=-1, keepdims=True)
    return ((xf - mu) * lax.rsqrt(var + LN_EPS)).astype(x.dtype) * g + b


def rms_norm(x, w):
    xf = x.astype(F32)
    return (xf * lax.rsqrt(jnp.mean(xf * xf, axis=-1, keepdims=True) + RMS_EPS)).astype(x.dtype) * w


def modulate(x, shift, scale):
    return x * (1 + scale[:, None, :]) + shift[:, None, :]


def causal_depthwise_conv(u, w, b):
    ch = u.shape[-1]
    y = lax.conv_general_dilated(
        u, w[:, None, :].astype(u.dtype), window_strides=(1,),
        padding=[(SSM_CONV - 1, 0)], dimension_numbers=('NWC', 'WIO', 'NWC'),
        feature_group_count=ch)
    return y + b


def ssd_chunked_scan(xs, dt, A, Bm, Cm):
    bsz, L, H, P = xs.shape
    G, N = SSM_GROUPS, SSM_STATE
    R = H // G
    nc = L // CHUNK
    xdt = (xs.astype(F32) * dt[..., None]).reshape(bsz, nc, CHUNK, G, R, P)
    dA = (dt * A).reshape(bsz, nc, CHUNK, G, R)
    Bc = Bm.astype(F32).reshape(bsz, nc, CHUNK, G, N)
    Cc = Cm.astype(F32).reshape(bsz, nc, CHUNK, G, N)
    causal = jnp.tril(jnp.ones((CHUNK, CHUNK), dtype=bool))

    def step(state, inp):
        xdt_c, dA_c, B_c, C_c = inp
        acum = jnp.cumsum(dA_c, axis=1)
        seg = acum[:, :, None] - acum[:, None, :]
        decay = jnp.exp(jnp.where(causal[None, :, :, None, None], seg, -jnp.inf))
        cb = jnp.einsum('blgn,bsgn->blsg', C_c, B_c)
        y_diag = jnp.einsum('blsg,blsgr,bsgrp->blgrp', cb, decay, xdt_c)
        y_off = jnp.einsum('blgn,bgrpn,blgr->blgrp', C_c, state, jnp.exp(acum))
        to_end = jnp.exp(acum[:, -1:] - acum)
        new_state = (state * jnp.exp(acum[:, -1])[..., None, None]
                     + jnp.einsum('bsgn,bsgr,bsgrp->bgrpn', B_c, to_end, xdt_c))
        return new_state, y_diag + y_off

    init = jnp.zeros((bsz, G, R, P, N), F32)
    seq_first = lambda a: jnp.moveaxis(a, 1, 0)
    _, ys = lax.scan(step, init, (seq_first(xdt), seq_first(dA), seq_first(Bc), seq_first(Cc)))
    return jnp.moveaxis(ys, 0, 1).reshape(bsz, L, H, P).astype(xs.dtype)


def mamba2_mixer(h, w_in, conv_w, conv_b, dt_bias, a_log, d_skip, norm_w, w_out):
    bsz, L, _ = h.shape
    proj = h @ w_in
    z = proj[..., :SSM_D_INNER]
    xbc = proj[..., SSM_D_INNER:SSM_D_INNER + SSM_CONV_DIM]
    dt_raw = proj[..., SSM_D_INNER + SSM_CONV_DIM:]
    xbc = jax.nn.silu(causal_depthwise_conv(xbc, conv_w, conv_b))
    xs = xbc[..., :SSM_D_INNER].reshape(bsz, L, SSM_HEADS, SSM_HEAD_DIM)
    Bm = xbc[..., SSM_D_INNER:SSM_D_INNER + SSM_GN].reshape(bsz, L, SSM_GROUPS, SSM_STATE)
    Cm = xbc[..., SSM_D_INNER + SSM_GN:].reshape(bsz, L, SSM_GROUPS, SSM_STATE)
    dt = jax.nn.softplus((dt_raw + dt_bias).astype(F32))
    A = -jnp.exp(a_log.astype(F32))
    y = ssd_chunked_scan(xs, dt, A, Bm, Cm)
    y = y + d_skip[:, None] * xs
    y = y.reshape(bsz, L, SSM_D_INNER) * jax.nn.silu(z)
    y = rms_norm(y, norm_w)
    return y @ w_out


def shared_kv(x_a, cond, kv_ada_w, kv_ada_b, kv_w, kv_fb):
    bsz, L, _ = x_a.shape
    shift, scale = jnp.split(cond @ kv_ada_w + kv_ada_b, 2, axis=-1)
    u = modulate(x_a, shift, scale)
    proj = u @ kv_w
    k = proj[..., :D_MODEL].reshape(bsz, L, ATTN_HEADS, ATTN_HEAD_DIM)
    v = proj[..., D_MODEL:2 * D_MODEL].reshape(bsz, L, ATTN_HEADS, ATTN_HEAD_DIM)
    log_f = jax.nn.log_sigmoid((proj[..., 2 * D_MODEL:] + kv_fb).astype(F32))
    cum_f = jnp.cumsum(log_f, axis=1)
    return k, v, jnp.moveaxis(cum_f, 2, 1)


def forgetting_attention(h, w_q, w_o, k, v, cum_f):
    bsz, L, _ = h.shape
    q = (h @ w_q).reshape(bsz, L, ATTN_HEADS, ATTN_HEAD_DIM)
    scale = ATTN_HEAD_DIM ** -0.5
    outs = []
    for i in range(L // Q_BLOCK):
        q0, q1 = i * Q_BLOCK, (i + 1) * Q_BLOCK
        logits = jnp.einsum('bqhd,bkhd->bhqk', q[:, q0:q1], k[:, :q1]).astype(F32) * scale
        logits = logits + cum_f[:, :, q0:q1, None] - cum_f[:, :, None, :q1]
        mask = (q0 + jnp.arange(Q_BLOCK))[:, None] >= jnp.arange(q1)[None, :]
        logits = jnp.where(mask, logits, -jnp.inf)
        p = jax.nn.softmax(logits, axis=-1).astype(v.dtype)
        outs.append(jnp.einsum('bhqk,bkhd->bqhd', p, v[:, :q1]))
    o = jnp.concatenate(outs, axis=1).reshape(bsz, L, D_MODEL)
    return o @ w_o


def moe_ffn(h, w_router, router_bias, w1, w3, w2, ws1, ws3, ws2):
    bsz, L, dm = h.shape
    t = h.reshape(-1, dm)
    T = t.shape[0]
    scores = jax.nn.sigmoid((t @ w_router).astype(F32))
    sel = scores + router_bias.astype(F32)
    grp = sel.reshape(T, N_EXPERT_GROUPS, N_EXPERTS // N_EXPERT_GROUPS)
    grp_score = jnp.sum(lax.top_k(grp, 2)[0], axis=-1)
    _, gidx = lax.top_k(grp_score, TOPK_GROUPS)
    gmask = jnp.sum(jax.nn.one_hot(gidx, N_EXPERT_GROUPS, dtype=F32), axis=1)
    emask = jnp.repeat(gmask, N_EXPERTS // N_EXPERT_GROUPS, axis=1) > 0
    _, eidx = lax.top_k(jnp.where(emask, sel, -jnp.inf), TOP_K)
    wsel = jnp.take_along_axis(scores, eidx, axis=1)
    wsel = wsel / jnp.sum(wsel, axis=-1, keepdims=True) * ROUTED_SCALE
    gate = jnp.einsum('tk,tke->te', wsel, jax.nn.one_hot(eidx, N_EXPERTS, dtype=F32)).astype(h.dtype)
    out = (jax.nn.silu(t @ ws1) * (t @ ws3)) @ ws2
    for e0 in range(0, N_EXPERTS, EXPERT_BLOCK):
        sl = slice(e0, e0 + EXPERT_BLOCK)
        a = jax.nn.silu(jnp.einsum('td,edf->tef', t, w1[sl])) * jnp.einsum('td,edf->tef', t, w3[sl])
        out = out + jnp.einsum('tef,efd->td', a * gate[:, sl, None], w2[sl])
    return out.reshape(bsz, L, dm)


def setup_inputs(seed: int = 0) -> dict:
    key = jax.random.key(seed)
    ks = iter(jax.random.split(key, 48))
    D = D_MODEL
    fan = D ** -0.5

    def nrm(shape, scale):
        return jax.random.normal(next(ks), shape, F32) * scale

    x = nrm((BATCH, SEQ, D), 1.0)
    c = nrm((BATCH, D), 1.0)
    ada_w = nrm((DEPTH, D, 6 * D), 0.2 * fan)
    ada_b = nrm((DEPTH, 6 * D), 0.02)
    ln1_g = 1.0 + nrm((DEPTH, D), 0.05)
    ln1_b = nrm((DEPTH, D), 0.02)
    ln2_g = 1.0 + nrm((DEPTH, D), 0.05)
    ln2_b = nrm((DEPTH, D), 0.02)
    ssm_w_in = nrm((N_A_LAYERS, D, SSM_IN_DIM), fan)
    ssm_conv_w = nrm((N_A_LAYERS, SSM_CONV, SSM_CONV_DIM), SSM_CONV ** -0.5)
    ssm_conv_b = nrm((N_A_LAYERS, SSM_CONV_DIM), 0.02)
    dt0 = jnp.exp(jax.random.uniform(next(ks), (N_A_LAYERS, SSM_HEADS), F32,
                                     minval=math.log(1e-3), maxval=math.log(1e-1)))
    ssm_dt_bias = dt0 + jnp.log(-jnp.expm1(-dt0))
    ssm_a_log = jnp.log(jax.random.uniform(next(ks), (N_A_LAYERS, SSM_HEADS), F32, minval=1.0, maxval=16.0))
    ssm_d = 1.0 + nrm((N_A_LAYERS, SSM_HEADS), 0.1)
    ssm_norm_w = 1.0 + nrm((N_A_LAYERS, SSM_D_INNER), 0.05)
    ssm_w_out = nrm((N_A_LAYERS, SSM_D_INNER, D), SSM_D_INNER ** -0.5 * DN_BETA)
    kv_ada_w = nrm((D, 2 * D), 0.2 * fan)
    kv_ada_b = nrm((2 * D,), 0.02)
    kv_w = jnp.concatenate([nrm((D, D), fan), nrm((D, D), fan * DN_BETA),
                            nrm((D, ATTN_HEADS), fan)], axis=-1)
    kv_fb = jax.random.uniform(next(ks), (ATTN_HEADS,), F32, minval=1.0, maxval=5.0)
    attn_w_q = nrm((N_B_LAYERS, D, D), fan)
    attn_w_o = nrm((N_B_LAYERS, D, D), fan * DN_BETA)
    moe_w_router = nrm((DEPTH, D, N_EXPERTS), fan)
    moe_bias = nrm((DEPTH, N_EXPERTS), 0.01)
    moe_w1 = nrm((DEPTH, N_EXPERTS, D, EXPERT_DIM), fan)
    moe_w3 = nrm((DEPTH, N_EXPERTS, D, EXPERT_DIM), fan)
    moe_w2 = nrm((DEPTH, N_EXPERTS, EXPERT_DIM, D), EXPERT_DIM ** -0.5 * DN_BETA)
    moe_ws1 = nrm((DEPTH, D, SHARED_DIM), fan)
    moe_ws3 = nrm((DEPTH, D, SHARED_DIM), fan)
    moe_ws2 = nrm((DEPTH, SHARED_DIM, D), SHARED_DIM ** -0.5 * DN_BETA)
    return {"x": x, "c": c, "ada_w": ada_w, "ada_b": ada_b,
            "ln1_g": ln1_g, "ln1_b": ln1_b, "ln2_g": ln2_g, "ln2_b": ln2_b,
            "ssm_w_in": ssm_w_in, "ssm_conv_w": ssm_conv_w, "ssm_conv_b": ssm_conv_b,
            "ssm_dt_bias": ssm_dt_bias, "ssm_a_log": ssm_a_log, "ssm_d": ssm_d,
            "ssm_norm_w": ssm_norm_w, "ssm_w_out": ssm_w_out,
            "kv_ada_w": kv_ada_w, "kv_ada_b": kv_ada_b, "kv_w": kv_w, "kv_fb": kv_fb,
            "attn_w_q": attn_w_q, "attn_w_o": attn_w_o,
            "moe_w_router": moe_w_router, "moe_bias": moe_bias,
            "moe_w1": moe_w1, "moe_w3": moe_w3, "moe_w2": moe_w2,
            "moe_ws1": moe_ws1, "moe_ws3": moe_ws3, "moe_ws2": moe_ws2}


def reference(x, c, ada_w, ada_b, ln1_g, ln1_b, ln2_g, ln2_b,
              ssm_w_in, ssm_conv_w, ssm_conv_b, ssm_dt_bias, ssm_a_log, ssm_d,
              ssm_norm_w, ssm_w_out, kv_ada_w, kv_ada_b, kv_w, kv_fb,
              attn_w_q, attn_w_o, moe_w_router, moe_bias,
              moe_w1, moe_w3, moe_w2, moe_ws1, moe_ws3, moe_ws2):
    cond = jax.nn.silu(c)
    k_sh = v_sh = cf_sh = None
    for layer in range(DEPTH):
        mod = cond @ ada_w[layer] + ada_b[layer]
        shift1, scale1, gate1, shift2, scale2, gate2 = jnp.split(mod, 6, axis=-1)
        h = modulate(x, shift1, scale1)
        if layer < N_A_LAYERS:
            a = layer
            y = mamba2_mixer(h, ssm_w_in[a], ssm_conv_w[a], ssm_conv_b[a], ssm_dt_bias[a],
                             ssm_a_log[a], ssm_d[a], ssm_norm_w[a], ssm_w_out[a])
        else:
            b = layer - N_A_LAYERS
            y = forgetting_attention(h, attn_w_q[b], attn_w_o[b], k_sh, v_sh, cf_sh)
        x = layer_norm(DN_ALPHA * x + (1 + gate1[:, None, :]) * y, ln1_g[layer], ln1_b[layer])
        h = modulate(x, shift2, scale2)
        y = moe_ffn(h, moe_w_router[layer], moe_bias[layer], moe_w1[layer], moe_w3[layer],
                    moe_w2[layer], moe_ws1[layer], moe_ws3[layer], moe_ws2[layer])
        x = layer_norm(DN_ALPHA * x + (1 + gate2[:, None, :]) * y, ln2_g[layer], ln2_b[layer])
        if layer == N_A_LAYERS - 1:
            k_sh, v_sh, cf_sh = shared_kv(x, cond, kv_ada_w, kv_ada_b, kv_w, kv_fb)
    return x
```

```python
import functools

import jax
import jax.numpy as jnp
from jax import lax
from jax.experimental import pallas as pl
from jax.experimental.pallas import tpu as pltpu

F32 = jnp.float32
BF16 = jnp.bfloat16
HIGHEST = lax.Precision.HIGHEST

DEPTH = 4
N_A_LAYERS = DEPTH // 2

SSM_HEAD_DIM = 64
SSM_GROUPS = 4
SSM_STATE = 128
SSM_CONV = 4

ATTN_HEAD_DIM = 64

N_EXPERTS = 64
TOP_K = 8
N_EXPERT_GROUPS = 8
TOPK_GROUPS = 4
ROUTED_SCALE = 2.5

DN_ALPHA = (2.0 * DEPTH) ** 0.25
LN_EPS = 1e-5
RMS_EPS = 1e-5

LANES = 128
SUBLANES = 8
VMEM_LIMIT = 56 * 1024 * 1024

SSD_CHUNK = 128
PROJ_ROWS = 256
ATTN_Q_ROWS = 256
ATTN_K_ROWS = 512
MOE_ROWS = 512
MOE_EXPERT_BLOCK = 8


def _sigmoid(v):
    return 1.0 / (1.0 + jnp.exp(-v))


def _silu(v):
    return v * _sigmoid(v)


def _layer_norm(r, g, b):
    mu = jnp.mean(r, axis=-1, keepdims=True)
    d = r - mu
    var = jnp.mean(d * d, axis=-1, keepdims=True)
    return d * lax.rsqrt(var + LN_EPS) * g + b


def _dot(a, b):
    return jnp.dot(a, b, preferred_element_type=F32)


def _dot_nt(a, b, precision=None):
    return lax.dot_general(a, b, (((1,), (1,)), ((), ())), preferred_element_type=F32, precision=precision)


def _dot_tn(a, b, precision=None):
    return lax.dot_general(a, b, (((0,), (0,)), ((), ())), preferred_element_type=F32, precision=precision)


def _params(*sem):
    return pltpu.CompilerParams(dimension_semantics=sem, vmem_limit_bytes=VMEM_LIMIT)


def _adaln_body(c_ref, w_ref, b_ref, o_ref):
    cond = _silu(c_ref[...])
    o_ref[...] = jnp.dot(cond, w_ref[...], precision=HIGHEST, preferred_element_type=F32) + b_ref[...]


def _adaln(c, w, b):
    nl, d, n = w.shape
    bsz = c.shape[0]
    tn = 1024
    return pl.pallas_call(
        _adaln_body,
        grid=(nl, n // tn),
        in_specs=[
            pl.BlockSpec((bsz, d), lambda l, j: (0, 0)),
            pl.BlockSpec((None, d, tn), lambda l, j: (l, 0, j)),
            pl.BlockSpec((None, 1, tn), lambda l, j: (l, 0, j)),
        ],
        out_specs=pl.BlockSpec((None, bsz, tn), lambda l, j: (l, 0, j)),
        out_shape=jax.ShapeDtypeStruct((nl, bsz, n), F32),
        compiler_params=_params("arbitrary", "arbitrary"),
        name="adaln",
    )(c, w, b.reshape(nl, 1, n))


def _ssm_in_body(x_ref, sh_ref, sc_ref, w_ref, z_ref, xbc_ref, dt_ref, *, d_inner, conv_dim, heads):
    h = (x_ref[...] * (1.0 + sc_ref[...]) + sh_ref[...]).astype(BF16)
    z_ref[...] = _dot(h, w_ref[:, 0:d_inner]).astype(BF16)
    xbc_ref[...] = _dot(h, w_ref[:, d_inner:d_inner + conv_dim]).astype(BF16)
    dt_ref[...] = _dot(h, w_ref[:, d_inner + conv_dim:])[:, :heads]


def _ssm_in(x, shift, scale, w_in, *, d_inner, conv_dim, heads):
    bsz, seq, d = x.shape
    tm = min(PROJ_ROWS, seq)
    n_in = d_inner + conv_dim + heads
    n_pad = -n_in % LANES
    w = jnp.pad(w_in.astype(BF16), ((0, 0), (0, n_pad)))
    vec = pl.BlockSpec((None, 1, d), lambda b, i: (b, 0, 0))
    return pl.pallas_call(
        functools.partial(_ssm_in_body, d_inner=d_inner, conv_dim=conv_dim, heads=heads),
        grid=(bsz, seq // tm),
        in_specs=[
            pl.BlockSpec((None, tm, d), lambda b, i: (b, i, 0)),
            vec, vec,
            pl.BlockSpec((d, n_in + n_pad), lambda b, i: (0, 0)),
        ],
        out_specs=[
            pl.BlockSpec((None, tm, d_inner), lambda b, i: (b, i, 0)),
            pl.BlockSpec((None, tm, conv_dim), lambda b, i: (b, i, 0)),
            pl.BlockSpec((None, tm, heads), lambda b, i: (b, i, 0)),
        ],
        out_shape=[
            jax.ShapeDtypeStruct((bsz, seq, d_inner), BF16),
            jax.ShapeDtypeStruct((bsz, seq, conv_dim), BF16),
            jax.ShapeDtypeStruct((bsz, seq, heads), F32),
        ],
        compiler_params=_params("arbitrary", "arbitrary"),
        name="ssm_in",
    )(x, shift, scale, w)


def _ssm_core_body(z_ref, xbc_ref, dt_ref, x_ref, g1_ref, cw_ref, cb_ref, dtb_ref, alog_ref, dskip_ref,
                   nw_ref, wout_ref, lng_ref, lnb_ref, o_ref, xpad, state, ybuf, *, heads, d_inner):
    q = SSD_CHUNK
    p_dim, n_dim = SSM_HEAD_DIM, SSM_STATE
    gn = SSM_GROUPS * n_dim
    hpg = heads // SSM_GROUPS
    tail = SUBLANES

    @pl.when(pl.program_id(1) == 0)
    def _():
        xpad[0:tail, :] = jnp.zeros((tail, xpad.shape[1]), F32)
        state[...] = jnp.zeros(state.shape, F32)

    xpad[tail:tail + q, :] = xbc_ref[...].astype(F32)
    acc = cb_ref[...] + cw_ref[SSM_CONV - 1:SSM_CONV, :] * xpad[tail:tail + q, :]
    for k in range(SSM_CONV - 1):
        off = tail - (SSM_CONV - 1) + k
        acc = acc + cw_ref[k:k + 1, :] * xpad[off:off + q, :]
    xpad[0:tail, :] = xpad[q:q + tail, :]
    act = _silu(acc)

    dt = dt_ref[...] + dtb_ref[...]
    dt = jnp.maximum(dt, 0.0) + jnp.log1p(jnp.exp(-jnp.abs(dt)))
    d_a = dt * (-jnp.exp(alog_ref[...]))
    row = lax.broadcasted_iota(jnp.int32, (q, q), 0)
    col = lax.broadcasted_iota(jnp.int32, (q, q), 1)
    causal = row >= col
    acum = jnp.dot(causal.astype(F32), d_a, precision=HIGHEST, preferred_element_type=F32)
    acum_t = acum.T
    dt_t = dt.T
    e_acum = jnp.exp(acum)
    w_end = jnp.exp(acum[q - 1:q, :] - acum) * dt
    e_last = jnp.exp(acum[q - 1:q, :])

    for g in range(SSM_GROUPS):
        b_g = act[:, d_inner + g * n_dim:d_inner + (g + 1) * n_dim]
        c_g = act[:, d_inner + gn + g * n_dim:d_inner + gn + (g + 1) * n_dim].astype(BF16)
        cb = _dot_nt(c_g, b_g.astype(BF16))
        for r in range(hpg):
            h = g * hpg + r
            xs_h = act[:, h * p_dim:(h + 1) * p_dim]
            xs_b = xs_h.astype(BF16)
            seg = acum[:, h:h + 1] - acum_t[h:h + 1, :]
            decay = jnp.exp(jnp.where(causal, seg, -jnp.inf))
            m = (cb * decay * dt_t[h:h + 1, :]).astype(BF16)
            st = state[h]
            y = _dot(m, xs_b)
            y = y + _dot_nt(c_g, st.astype(BF16)) * e_acum[:, h:h + 1]
            bw = (b_g * w_end[:, h:h + 1]).astype(BF16)
            state[h] = st * e_last[:, h:h + 1] + _dot_tn(xs_b, bw)
            ybuf[:, h * p_dim:(h + 1) * p_dim] = y + dskip_ref[h] * xs_h

    y = ybuf[...] * _silu(z_ref[...].astype(F32))
    y = y * lax.rsqrt(jnp.mean(y * y, axis=-1, keepdims=True) + RMS_EPS) * nw_ref[...]
    out = _dot(y.astype(BF16), wout_ref[...])
    r_sum = DN_ALPHA * x_ref[...] + (1.0 + g1_ref[...]) * out
    o_ref[...] = _layer_norm(r_sum, lng_ref[...], lnb_ref[...])


def _ssm_core(z, xbc, dt_raw, x, gate1, conv_w, conv_b, dt_bias, a_log, d_skip, norm_w, w_out, ln_g, ln_b):
    bsz, seq, d = x.shape
    d_inner = z.shape[-1]
    conv_dim = xbc.shape[-1]
    heads = dt_raw.shape[-1]
    q = SSD_CHUNK
    rows = lambda n: pl.BlockSpec((None, q, n), lambda b, i: (b, i, 0))
    full = lambda a: pl.BlockSpec(a.shape, lambda b, i: (0,) * a.ndim)
    conv_b, dt_bias, a_log = conv_b[None], dt_bias[None], a_log[None]
    norm_w, ln_g, ln_b = norm_w[None], ln_g[None], ln_b[None]
    w_out = w_out.astype(BF16)
    return pl.pallas_call(
        functools.partial(_ssm_core_body, heads=heads, d_inner=d_inner),
        grid=(bsz, seq // q),
        in_specs=[
            rows(d_inner), rows(conv_dim), rows(heads), rows(d),
            pl.BlockSpec((None, 1, d), lambda b, i: (b, 0, 0)),
            full(conv_w), full(conv_b), full(dt_bias), full(a_log),
            pl.BlockSpec(memory_space=pltpu.SMEM),
            full(norm_w), full(w_out), full(ln_g), full(ln_b),
        ],
        out_specs=rows(d),
        out_shape=jax.ShapeDtypeStruct((bsz, seq, d), F32),
        scratch_shapes=[
            pltpu.VMEM((q + SUBLANES, conv_dim), F32),
            pltpu.VMEM((heads, SSM_HEAD_DIM, SSM_STATE), F32),
            pltpu.VMEM((q, d_inner), F32),
        ],
        compiler_params=_params("arbitrary", "arbitrary"),
        name="ssm_core",
    )(z, xbc, dt_raw, x, gate1, conv_w, conv_b, dt_bias, a_log, d_skip, norm_w, w_out, ln_g, ln_b)


def _kv_body(x_ref, sh_ref, sc_ref, w_ref, fb_ref, k_ref, v_ref, cf_ref, cft_ref, carry, *, d, heads):
    @pl.when(pl.program_id(1) == 0)
    def _():
        carry[...] = jnp.zeros(carry.shape, F32)

    u = (x_ref[...] * (1.0 + sc_ref[...]) + sh_ref[...]).astype(BF16)
    k_ref[...] = _dot(u, w_ref[:, 0:d]).astype(BF16)
    v_ref[...] = _dot(u, w_ref[:, d:2 * d]).astype(BF16)
    f = _dot(u, w_ref[:, 2 * d:])[:, :heads] + fb_ref[...]
    log_f = jnp.minimum(f, 0.0) - jnp.log1p(jnp.exp(-jnp.abs(f)))
    lt = log_f.T
    tm = lt.shape[1]
    lane = lax.broadcasted_iota(jnp.int32, lt.shape, 1)
    step = 1
    while step < tm:
        lt = lt + jnp.where(lane >= step, pltpu.roll(lt, step, axis=1), 0.0)
        step *= 2
    lt = lt + carry[...]
    carry[...] = lt[:, tm - 1:tm]
    cft_ref[...] = lt
    cf_ref[...] = lt.T


def _shared_kv(x, shift, scale, kv_w, kv_fb):
    bsz, seq, d = x.shape
    heads = kv_w.shape[1] - 2 * d
    tm = min(PROJ_ROWS, seq)
    n_pad = -kv_w.shape[1] % LANES
    w = jnp.pad(kv_w.astype(BF16), ((0, 0), (0, n_pad)))
    vec = pl.BlockSpec((None, 1, d), lambda b, i: (b, 0, 0))
    rows = lambda n: pl.BlockSpec((None, tm, n), lambda b, i: (b, i, 0))
    return pl.pallas_call(
        functools.partial(_kv_body, d=d, heads=heads),
        grid=(bsz, seq // tm),
        in_specs=[rows(d), vec, vec,
                  pl.BlockSpec(w.shape, lambda b, i: (0, 0)),
                  pl.BlockSpec((1, heads), lambda b, i: (0, 0))],
        out_specs=[rows(d), rows(d), rows(heads),
                   pl.BlockSpec((None, heads, tm), lambda b, i: (b, 0, i))],
        out_shape=[
            jax.ShapeDtypeStruct((bsz, seq, d), BF16),
            jax.ShapeDtypeStruct((bsz, seq, d), BF16),
            jax.ShapeDtypeStruct((bsz, seq, heads), F32),
            jax.ShapeDtypeStruct((bsz, heads, seq), F32),
        ],
        scratch_shapes=[pltpu.VMEM((heads, 1), F32)],
        compiler_params=_params("arbitrary", "arbitrary"),
        name="shared_kv",
    )(x, shift, scale, w, kv_fb[None])


def _attn_body(x_ref, sh_ref, sc_ref, g1_ref, wq_ref, wo_ref, k_ref, v_ref, cfq_ref, cfk_ref, lng_ref, lnb_ref,
               o_ref, q_s, m_s, l_s, acc_s, *, heads, tq, tk):
    i = pl.program_id(1)
    j = pl.program_id(2)
    hd = ATTN_HEAD_DIM

    @pl.when(j == 0)
    def _():
        h = (x_ref[...] * (1.0 + sc_ref[...]) + sh_ref[...]).astype(BF16)
        q_s[...] = (_dot(h, wq_ref[...]) * (hd ** -0.5)).astype(BF16)
        m_s[...] = jnp.full(m_s.shape, -jnp.inf, F32)
        l_s[...] = jnp.zeros(l_s.shape, F32)
        acc_s[...] = jnp.zeros(acc_s.shape, F32)

    @pl.when(j * tk <= i * tq + (tq - 1))
    def _():
        row = i * tq + lax.broadcasted_iota(jnp.int32, (tq, tk), 0)
        col = j * tk + lax.broadcasted_iota(jnp.int32, (tq, tk), 1)
        mask = row >= col
        cfq = cfq_ref[...]
        cfk = cfk_ref[...]
        for h in range(heads):
            sl = slice(h * hd, (h + 1) * hd)
            s = _dot_nt(q_s[:, sl], k_ref[:, sl])
            s = s + (cfq[:, h:h + 1] - cfk[h:h + 1, :])
            s = jnp.where(mask, s, -jnp.inf)
            m_prev = m_s[h]
            m_new = jnp.maximum(m_prev, jnp.max(s, axis=1, keepdims=True))
            alpha = jnp.exp(m_prev - m_new)
            p = jnp.exp(s - m_new)
            l_s[h] = alpha * l_s[h] + jnp.sum(p, axis=1, keepdims=True)
            acc_s[:, sl] = alpha * acc_s[:, sl] + _dot(p.astype(BF16), v_ref[:, sl])
            m_s[h] = m_new

    @pl.when(j == pl.num_programs(2) - 1)
    def _():
        for h in range(heads):
            sl = slice(h * hd, (h + 1) * hd)
            acc_s[:, sl] = acc_s[:, sl] / l_s[h]
        y = _dot(acc_s[...].astype(BF16), wo_ref[...])
        r_sum = DN_ALPHA * x_ref[...] + (1.0 + g1_ref[...]) * y
        o_ref[...] = _layer_norm(r_sum, lng_ref[...], lnb_ref[...])


def _attention(x, shift, scale, gate1, w_q, w_o, k, v, cf, cf_t, ln_g, ln_b):
    bsz, seq, d = x.shape
    heads = cf.shape[-1]
    tq = min(ATTN_Q_ROWS, seq)
    tk = min(ATTN_K_ROWS, seq)
    last_k = lambda i: (i * tq + tq - 1) // tk
    vec = pl.BlockSpec((None, 1, d), lambda b, i, j: (b, 0, 0))
    full = lambda a: pl.BlockSpec(a.shape, lambda b, i, j: (0,) * a.ndim)
    kv_spec = pl.BlockSpec((None, tk, d), lambda b, i, j: (b, jnp.minimum(j, last_k(i)), 0))
    w_q, w_o, ln_g, ln_b = w_q.astype(BF16), w_o.astype(BF16), ln_g[None], ln_b[None]
    return pl.pallas_call(
        functools.partial(_attn_body, heads=heads, tq=tq, tk=tk),
        grid=(bsz, seq // tq, seq // tk),
        in_specs=[
            pl.BlockSpec((None, tq, d), lambda b, i, j: (b, i, 0)),
            vec, vec, vec, full(w_q), full(w_o), kv_spec, kv_spec,
            pl.BlockSpec((None, tq, heads), lambda b, i, j: (b, i, 0)),
            pl.BlockSpec((None, heads, tk), lambda b, i, j: (b, 0, jnp.minimum(j, last_k(i)))),
            full(ln_g), full(ln_b),
        ],
        out_specs=pl.BlockSpec((None, tq, d), lambda b, i, j: (b, i, 0)),
        out_shape=jax.ShapeDtypeStruct((bsz, seq, d), F32),
        scratch_shapes=[
            pltpu.VMEM((tq, d), BF16),
            pltpu.VMEM((heads, tq, 1), F32),
            pltpu.VMEM((heads, tq, 1), F32),
            pltpu.VMEM((tq, d), F32),
        ],
        compiler_params=_params("arbitrary", "arbitrary", "arbitrary"),
        name="fox_attention",
    )(x, shift, scale, gate1, w_q, w_o, k, v, cf, cf_t, ln_g, ln_b)


def _route(sel, scores):
    n_e, t = sel.shape
    per = n_e // N_EXPERT_GROUPS
    sub = lax.broadcasted_iota(jnp.int32, (per, t), 0)
    neg = -jnp.inf
    gs = jnp.zeros((N_EXPERT_GROUPS, t), F32)
    gidx = lax.broadcasted_iota(jnp.int32, (N_EXPERT_GROUPS, t), 0)
    for g in range(N_EXPERT_GROUPS):
        v = sel[g * per:(g + 1) * per, :]
        m1 = jnp.max(v, axis=0, keepdims=True)
        first = jnp.min(jnp.where(v == m1, sub, per), axis=0, keepdims=True)
        m2 = jnp.max(jnp.where(sub == first, neg, v), axis=0, keepdims=True)
        gs = jnp.where(gidx == g, m1 + m2, gs)
    grank = jnp.zeros((N_EXPERT_GROUPS, t), jnp.int32)
    for g in range(N_EXPERT_GROUPS):
        other = gs[g:g + 1, :]
        beats = (other > gs) | ((other >= gs) & (gidx > g))
        grank = grank + jnp.where(beats, 1, 0)
    masked = jnp.concatenate(
        [jnp.where(grank[g:g + 1, :] < TOPK_GROUPS, sel[g * per:(g + 1) * per, :], neg)
         for g in range(N_EXPERT_GROUPS)], axis=0)
    eidx = lax.broadcasted_iota(jnp.int32, (n_e, t), 0)
    erank = jnp.zeros((n_e, t), jnp.int32)
    for e in range(n_e):
        other = masked[e:e + 1, :]
        beats = (other > masked) | ((other >= masked) & (eidx > e))
        erank = erank + jnp.where(beats, 1, 0)
    w = jnp.where(erank < TOP_K, scores, 0.0)
    return w / jnp.sum(w, axis=0, keepdims=True) * ROUTED_SCALE


def _moe_body(x_ref, sh_ref, sc_ref, g2_ref, wr_ref, rb_ref, w1_ref, w3_ref, w2_ref, ws1_ref, ws3_ref, ws2_ref,
              lng_ref, lnb_ref, o_ref, h_s, gate_s, acc_s, *, eb, f_dim):
    j = pl.program_id(2)

    @pl.when(j == 0)
    def _():
        h = x_ref[...] * (1.0 + sc_ref[...]) + sh_ref[...]
        hb = h.astype(BF16)
        h_s[...] = hb
        scores = _sigmoid(_dot_nt(wr_ref[...], h, precision=HIGHEST))
        gate_s[...] = _route(scores + rb_ref[...], scores)
        a = _silu(_dot(hb, ws1_ref[...])) * _dot(hb, ws3_ref[...])
        acc_s[...] = _dot(a.astype(BF16), ws2_ref[...])

    hb = h_s[...]
    a = _silu(_dot(hb, w1_ref[...])) * _dot(hb, w3_ref[...])
    g = gate_s[pl.ds(pl.multiple_of(j * eb, eb), eb), :].astype(BF16)
    erow = lax.broadcasted_iota(jnp.int32, (eb, eb * f_dim), 0)
    ecol = lax.broadcasted_iota(jnp.int32, (eb, eb * f_dim), 1) // f_dim
    expand = (erow == ecol).astype(BF16)
    gexp = _dot_tn(g, expand)
    acc_s[...] += _dot((a * gexp).astype(BF16), w2_ref[...])

    @pl.when(j == pl.num_programs(2) - 1)
    def _():
        r_sum = DN_ALPHA * x_ref[...] + (1.0 + g2_ref[...]) * acc_s[...]
        o_ref[...] = _layer_norm(r_sum, lng_ref[...], lnb_ref[...])


def _moe(x, shift, scale, gate2, w_router, router_bias, w1, w3, w2, ws1, ws3, ws2, ln_g, ln_b):
    bsz, seq, d = x.shape
    n_e, _, f_dim = w1.shape
    tm = min(MOE_ROWS, seq)
    eb = MOE_EXPERT_BLOCK
    w1c = w1.astype(BF16).transpose(1, 0, 2).reshape(d, n_e * f_dim)
    w3c = w3.astype(BF16).transpose(1, 0, 2).reshape(d, n_e * f_dim)
    w2c = w2.astype(BF16).reshape(n_e * f_dim, d)
    wr_t = w_router.T
    rb = router_bias[:, None]
    ws1, ws3, ws2 = ws1.astype(BF16), ws3.astype(BF16), ws2.astype(BF16)
    ln_g, ln_b = ln_g[None], ln_b[None]
    vec = pl.BlockSpec((None, 1, d), lambda b, i, j: (b, 0, 0))
    full = lambda a: pl.BlockSpec(a.shape, lambda b, i, j: (0,) * a.ndim)
    return pl.pallas_call(
        functools.partial(_moe_body, eb=eb, f_dim=f_dim),
        grid=(bsz, seq // tm, n_e // eb),
        in_specs=[
            pl.BlockSpec((None, tm, d), lambda b, i, j: (b, i, 0)),
            vec, vec, vec, full(wr_t), full(rb),
            pl.BlockSpec((d, eb * f_dim), lambda b, i, j: (0, j)),
            pl.BlockSpec((d, eb * f_dim), lambda b, i, j: (0, j)),
            pl.BlockSpec((eb * f_dim, d), lambda b, i, j: (j, 0)),
            full(ws1), full(ws3), full(ws2), full(ln_g), full(ln_b),
        ],
        out_specs=pl.BlockSpec((None, tm, d), lambda b, i, j: (b, i, 0)),
        out_shape=jax.ShapeDtypeStruct((bsz, seq, d), F32),
        scratch_shapes=[
            pltpu.VMEM((tm, d), BF16),
            pltpu.VMEM((n_e, tm), F32),
            pltpu.VMEM((tm, d), F32),
        ],
        compiler_params=_params("arbitrary", "arbitrary", "arbitrary"),
        name="moe",
    )(x, shift, scale, gate2, wr_t, rb, w1c, w3c, w2c, ws1, ws3, ws2, ln_g, ln_b)


def kernel(x, c, ada_w, ada_b, ln1_g, ln1_b, ln2_g, ln2_b, ssm_w_in, ssm_conv_w, ssm_conv_b, ssm_dt_bias,
           ssm_a_log, ssm_d, ssm_norm_w, ssm_w_out, kv_ada_w, kv_ada_b, kv_w, kv_fb, attn_w_q, attn_w_o,
           moe_w_router, moe_bias, moe_w1, moe_w3, moe_w2, moe_ws1, moe_ws3, moe_ws2):
    d = x.shape[-1]
    mods = _adaln(c, ada_w, ada_b)
    kv_mod = _adaln(c, kv_ada_w[None], kv_ada_b[None])[0]
    part = lambda m, n: m[:, None, n * d:(n + 1) * d]
    heads = ssm_dt_bias.shape[-1]
    d_inner = ssm_norm_w.shape[-1]
    conv_dim = ssm_conv_w.shape[-1]
    k_sh = v_sh = cf = cf_t = None
    for layer in range(DEPTH):
        shift1, scale1, gate1, shift2, scale2, gate2 = (part(mods[layer], n) for n in range(6))
        if layer < N_A_LAYERS:
            a = layer
            z, xbc, dt_raw = _ssm_in(x, shift1, scale1, ssm_w_in[a], d_inner=d_inner, conv_dim=conv_dim, heads=heads)
            x = _ssm_core(z, xbc, dt_raw, x, gate1, ssm_conv_w[a], ssm_conv_b[a], ssm_dt_bias[a], ssm_a_log[a],
                          ssm_d[a], ssm_norm_w[a], ssm_w_out[a], ln1_g[layer], ln1_b[layer])
        else:
            b = layer - N_A_LAYERS
            x = _attention(x, shift1, scale1, gate1, attn_w_q[b], attn_w_o[b], k_sh, v_sh, cf, cf_t,
                           ln1_g[layer], ln1_b[layer])
        x = _moe(x, shift2, scale2, gate2, moe_w_router[layer], moe_bias[layer], moe_w1[layer], moe_w3[layer],
                 moe_w2[layer], moe_ws1[layer], moe_ws3[layer], moe_ws2[layer], ln2_g[layer], ln2_b[layer])
        if layer == N_A_LAYERS - 1:
            k_sh, v_sh, cf, cf_t = _shared_kv(x, part(kv_mod, 0), part(kv_mod, 1), kv_w, kv_fb)
    return x
```

```python
import functools

import jax
import jax.numpy as jnp
from jax import lax
from jax.experimental import pallas as pl
from jax.experimental.pallas import tpu as pltpu

F32 = jnp.float32
BF16 = jnp.bfloat16
HIGHEST = lax.Precision.HIGHEST

DEPTH = 4
N_A_LAYERS = DEPTH // 2

SSM_HEAD_DIM = 64
SSM_GROUPS = 4
SSM_STATE = 128
SSM_CONV = 4

ATTN_HEAD_DIM = 64

N_EXPERTS = 64
TOP_K = 8
N_EXPERT_GROUPS = 8
TOPK_GROUPS = 4
ROUTED_SCALE = 2.5

DN_ALPHA = (2.0 * DEPTH) ** 0.25
LN_EPS = 1e-5
RMS_EPS = 1e-5

LANES = 128
SUBLANES = 8
VMEM_LIMIT = 56 * 1024 * 1024

SSD_CHUNK = 128
PROJ_ROWS = 256
ATTN_ROWS = 512
ATTN_HEAD_UNROLL = 8
MOE_ROWS = 512
MOE_EXPERT_BLOCK = 8


def _sigmoid(v):
    return 1.0 / (1.0 + jnp.exp(-v))


def _silu(v):
    return v * _sigmoid(v)


def _layer_norm(r, g, b):
    mu = jnp.mean(r, axis=-1, keepdims=True)
    d = r - mu
    var = jnp.mean(d * d, axis=-1, keepdims=True)
    return d * lax.rsqrt(var + LN_EPS) * g + b


def _dot(a, b):
    return jnp.dot(a, b, preferred_element_type=F32)


def _dot_nt(a, b, precision=None):
    return lax.dot_general(a, b, (((1,), (1,)), ((), ())), preferred_element_type=F32, precision=precision)


def _dot_tn(a, b, precision=None):
    return lax.dot_general(a, b, (((0,), (0,)), ((), ())), preferred_element_type=F32, precision=precision)


def _params(*sem):
    return pltpu.CompilerParams(dimension_semantics=sem, vmem_limit_bytes=VMEM_LIMIT)


def _adaln_body(c_ref, w_ref, b_ref, o_ref):
    cond = _silu(c_ref[...])
    o_ref[...] = jnp.dot(cond, w_ref[...], precision=HIGHEST, preferred_element_type=F32) + b_ref[...]


def _adaln(c, w, b):
    nl, d, n = w.shape
    bsz = c.shape[0]
    tn = 1024
    return pl.pallas_call(
        _adaln_body,
        grid=(nl, n // tn),
        in_specs=[
            pl.BlockSpec((bsz, d), lambda l, j: (0, 0)),
            pl.BlockSpec((None, d, tn), lambda l, j: (l, 0, j)),
            pl.BlockSpec((None, 1, tn), lambda l, j: (l, 0, j)),
        ],
        out_specs=pl.BlockSpec((None, bsz, tn), lambda l, j: (l, 0, j)),
        out_shape=jax.ShapeDtypeStruct((nl, bsz, n), F32),
        compiler_params=_params("arbitrary", "arbitrary"),
        name="adaln",
    )(c, w, b.reshape(nl, 1, n))


def _ssm_in_body(x_ref, sh_ref, sc_ref, w_ref, z_ref, xbc_ref, dt_ref, *, d_inner, conv_dim, heads):
    h = (x_ref[...] * (1.0 + sc_ref[...]) + sh_ref[...]).astype(BF16)
    z_ref[...] = _dot(h, w_ref[:, 0:d_inner]).astype(BF16)
    xbc_ref[...] = _dot(h, w_ref[:, d_inner:d_inner + conv_dim]).astype(BF16)
    dt_ref[...] = _dot(h, w_ref[:, d_inner + conv_dim:])[:, :heads]


def _ssm_in(x, shift, scale, w_in, *, d_inner, conv_dim, heads):
    bsz, seq, d = x.shape
    tm = min(PROJ_ROWS, seq)
    n_in = d_inner + conv_dim + heads
    n_pad = -n_in % LANES
    w = jnp.pad(w_in.astype(BF16), ((0, 0), (0, n_pad)))
    vec = pl.BlockSpec((None, 1, d), lambda b, i: (b, 0, 0))
    return pl.pallas_call(
        functools.partial(_ssm_in_body, d_inner=d_inner, conv_dim=conv_dim, heads=heads),
        grid=(bsz, seq // tm),
        in_specs=[
            pl.BlockSpec((None, tm, d), lambda b, i: (b, i, 0)),
            vec, vec,
            pl.BlockSpec((d, n_in + n_pad), lambda b, i: (0, 0)),
        ],
        out_specs=[
            pl.BlockSpec((None, tm, d_inner), lambda b, i: (b, i, 0)),
            pl.BlockSpec((None, tm, conv_dim), lambda b, i: (b, i, 0)),
            pl.BlockSpec((None, tm, heads), lambda b, i: (b, i, 0)),
        ],
        out_shape=[
            jax.ShapeDtypeStruct((bsz, seq, d_inner), BF16),
            jax.ShapeDtypeStruct((bsz, seq, conv_dim), BF16),
            jax.ShapeDtypeStruct((bsz, seq, heads), F32),
        ],
        compiler_params=_params("arbitrary", "arbitrary"),
        name="ssm_in",
    )(x, shift, scale, w)


def _ssm_core_body(z_ref, xbc_ref, dt_ref, x_ref, g1_ref, cw_ref, cb_ref, dtb_ref, alog_ref, dskip_ref,
                   nw_ref, wout_ref, lng_ref, lnb_ref, o_ref, xpad, state, ybuf, *, heads, d_inner):
    q = SSD_CHUNK
    p_dim, n_dim = SSM_HEAD_DIM, SSM_STATE
    gn = SSM_GROUPS * n_dim
    hpg = heads // SSM_GROUPS
    tail = SUBLANES

    @pl.when(pl.program_id(1) == 0)
    def _():
        xpad[0:tail, :] = jnp.zeros((tail, xpad.shape[1]), F32)
        state[...] = jnp.zeros(state.shape, F32)

    xpad[tail:tail + q, :] = xbc_ref[...].astype(F32)
    acc = cb_ref[...] + cw_ref[SSM_CONV - 1:SSM_CONV, :] * xpad[tail:tail + q, :]
    for k in range(SSM_CONV - 1):
        off = tail - (SSM_CONV - 1) + k
        acc = acc + cw_ref[k:k + 1, :] * xpad[off:off + q, :]
    xpad[0:tail, :] = xpad[q:q + tail, :]
    act = _silu(acc)

    dt = dt_ref[...] + dtb_ref[...]
    dt = jnp.maximum(dt, 0.0) + jnp.log1p(jnp.exp(-jnp.abs(dt)))
    d_a = dt * (-jnp.exp(alog_ref[...]))
    row = lax.broadcasted_iota(jnp.int32, (q, q), 0)
    col = lax.broadcasted_iota(jnp.int32, (q, q), 1)
    causal = row >= col
    acum = jnp.dot(causal.astype(F32), d_a, precision=HIGHEST, preferred_element_type=F32)
    acum_t = acum.T
    dt_t = dt.T
    e_acum = jnp.exp(acum)
    w_end = jnp.exp(acum[q - 1:q, :] - acum) * dt
    e_last = jnp.exp(acum[q - 1:q, :])

    for g in range(SSM_GROUPS):
        b_g = act[:, d_inner + g * n_dim:d_inner + (g + 1) * n_dim]
        c_g = act[:, d_inner + gn + g * n_dim:d_inner + gn + (g + 1) * n_dim].astype(BF16)
        cb = _dot_nt(c_g, b_g.astype(BF16))
        for r in range(hpg):
            h = g * hpg + r
            xs_h = act[:, h * p_dim:(h + 1) * p_dim]
            xs_b = xs_h.astype(BF16)
            seg = acum[:, h:h + 1] - acum_t[h:h + 1, :]
            decay = jnp.exp(jnp.where(causal, seg, -jnp.inf))
            m = (cb * decay * dt_t[h:h + 1, :]).astype(BF16)
            st = state[h]
            y = _dot(m, xs_b)
            y = y + _dot_nt(c_g, st.astype(BF16)) * e_acum[:, h:h + 1]
            bw = (b_g * w_end[:, h:h + 1]).astype(BF16)
            state[h] = st * e_last[:, h:h + 1] + _dot_tn(xs_b, bw)
            ybuf[:, h * p_dim:(h + 1) * p_dim] = y + dskip_ref[h] * xs_h

    y = ybuf[...] * _silu(z_ref[...].astype(F32))
    y = y * lax.rsqrt(jnp.mean(y * y, axis=-1, keepdims=True) + RMS_EPS) * nw_ref[...]
    out = _dot(y.astype(BF16), wout_ref[...])
    r_sum = DN_ALPHA * x_ref[...] + (1.0 + g1_ref[...]) * out
    o_ref[...] = _layer_norm(r_sum, lng_ref[...], lnb_ref[...])


def _ssm_core(z, xbc, dt_raw, x, gate1, conv_w, conv_b, dt_bias, a_log, d_skip, norm_w, w_out, ln_g, ln_b):
    bsz, seq, d = x.shape
    d_inner = z.shape[-1]
    conv_dim = xbc.shape[-1]
    heads = dt_raw.shape[-1]
    q = SSD_CHUNK
    rows = lambda n: pl.BlockSpec((None, q, n), lambda b, i: (b, i, 0))
    full = lambda a: pl.BlockSpec(a.shape, lambda b, i: (0,) * a.ndim)
    conv_b, dt_bias, a_log = conv_b[None], dt_bias[None], a_log[None]
    norm_w, ln_g, ln_b = norm_w[None], ln_g[None], ln_b[None]
    w_out = w_out.astype(BF16)
    return pl.pallas_call(
        functools.partial(_ssm_core_body, heads=heads, d_inner=d_inner),
        grid=(bsz, seq // q),
        in_specs=[
            rows(d_inner), rows(conv_dim), rows(heads), rows(d),
            pl.BlockSpec((None, 1, d), lambda b, i: (b, 0, 0)),
            full(conv_w), full(conv_b), full(dt_bias), full(a_log),
            pl.BlockSpec(memory_space=pltpu.SMEM),
            full(norm_w), full(w_out), full(ln_g), full(ln_b),
        ],
        out_specs=rows(d),
        out_shape=jax.ShapeDtypeStruct((bsz, seq, d), F32),
        scratch_shapes=[
            pltpu.VMEM((q + SUBLANES, conv_dim), F32),
            pltpu.VMEM((heads, SSM_HEAD_DIM, SSM_STATE), F32),
            pltpu.VMEM((q, d_inner), F32),
        ],
        compiler_params=_params("arbitrary", "arbitrary"),
        name="ssm_core",
    )(z, xbc, dt_raw, x, gate1, conv_w, conv_b, dt_bias, a_log, d_skip, norm_w, w_out, ln_g, ln_b)


AUG_LANES = LANES
LOG2E = 1.4426950408889634


def _split3(v):
    hi = v.astype(BF16)
    r = v - hi.astype(F32)
    mid = r.astype(BF16)
    lo = (r - mid.astype(F32)).astype(BF16)
    return hi, mid, lo


def _place(parts, first_lane, sign):
    heads = parts[0].shape[1]
    r = lax.broadcasted_iota(jnp.int32, (heads, heads * AUG_LANES), 0)
    c = lax.broadcasted_iota(jnp.int32, (heads, heads * AUG_LANES), 1)
    out = None
    for k, part in enumerate(parts):
        mat = jnp.where(c == r * AUG_LANES + (first_lane + k), sign, 0.0).astype(BF16)
        t = _dot(part, mat)
        out = t if out is None else out + t
    return out


def _head_window(m, h, hd):
    base = (h * hd // LANES) * LANES
    w = m[:, base:base + LANES]
    shift = (h * hd) % LANES
    return pltpu.roll(w, LANES - shift, axis=1) if shift else w


def _kv_body(x_ref, sh_ref, sc_ref, w_ref, fb_ref, k_ref, v_ref, cfp_ref, carry, *, d, heads):
    hd = ATTN_HEAD_DIM

    @pl.when(pl.program_id(1) == 0)
    def _():
        carry[...] = jnp.zeros(carry.shape, F32)

    u = (x_ref[...] * (1.0 + sc_ref[...]) + sh_ref[...]).astype(BF16)
    kf = _dot(u, w_ref[:, 0:d])
    vf = _dot(u, w_ref[:, d:2 * d])
    f = _dot(u, w_ref[:, 2 * d:])[:, :heads] + fb_ref[...]
    log_f = jnp.minimum(f, 0.0) - jnp.log1p(jnp.exp(-jnp.abs(f)))
    lt = log_f.T
    tm = lt.shape[1]
    lane_t = lax.broadcasted_iota(jnp.int32, lt.shape, 1)
    step = 1
    while step < tm:
        lt = lt + jnp.where(lane_t >= step, pltpu.roll(lt, step, axis=1), 0.0)
        step *= 2
    lt = lt + carry[...]
    carry[...] = lt[:, tm - 1:tm]
    parts = _split3(lt.T * LOG2E)
    for n in range(3):
        cfp_ref[n] = parts[n]
    extra = _place(parts, hd + 3, -1.0)
    lane = lax.broadcasted_iota(jnp.int32, (tm, AUG_LANES), 1)
    ones_k = jnp.where((lane >= hd) & (lane < hd + 3), 1.0, 0.0)
    ones_v = jnp.where(lane == hd, 1.0, 0.0)
    for h in range(heads):
        ext = extra[:, h * AUG_LANES:(h + 1) * AUG_LANES] + ones_k
        k_ref[h] = jnp.where(lane < hd, _head_window(kf, h, hd), ext).astype(BF16)
        v_ref[h] = jnp.where(lane < hd, _head_window(vf, h, hd), ones_v).astype(BF16)


def _shared_kv(x, shift, scale, kv_w, kv_fb):
    bsz, seq, d = x.shape
    heads = kv_w.shape[1] - 2 * d
    tm = min(PROJ_ROWS, seq)
    n_pad = -kv_w.shape[1] % LANES
    w = jnp.pad(kv_w.astype(BF16), ((0, 0), (0, n_pad)))
    vec = pl.BlockSpec((None, 1, d), lambda b, i: (b, 0, 0))
    aug = pl.BlockSpec((None, heads, tm, AUG_LANES), lambda b, i: (b, 0, i, 0))
    return pl.pallas_call(
        functools.partial(_kv_body, d=d, heads=heads),
        grid=(bsz, seq // tm),
        in_specs=[pl.BlockSpec((None, tm, d), lambda b, i: (b, i, 0)), vec, vec,
                  pl.BlockSpec(w.shape, lambda b, i: (0, 0)),
                  pl.BlockSpec((1, heads), lambda b, i: (0, 0))],
        out_specs=[aug, aug, pl.BlockSpec((None, 3, tm, heads), lambda b, i: (b, 0, i, 0))],
        out_shape=[
            jax.ShapeDtypeStruct((bsz, heads, seq, AUG_LANES), BF16),
            jax.ShapeDtypeStruct((bsz, heads, seq, AUG_LANES), BF16),
            jax.ShapeDtypeStruct((bsz, 3, seq, heads), BF16),
        ],
        scratch_shapes=[pltpu.VMEM((heads, 1), F32)],
        compiler_params=_params("arbitrary", "arbitrary"),
        name="shared_kv",
    )(x, shift, scale, w, kv_fb[None])


def _attn_block(q_s, k_ref, v_ref, m_s, acc_s, *, heads, t, diagonal):
    def head(h, carry):
        s = _dot_nt(q_s[h], k_ref[h])
        if diagonal:
            qi = lax.broadcasted_iota(jnp.int32, (t, t), 0)
            ki = lax.broadcasted_iota(jnp.int32, (t, t), 1)
            s = jnp.where(qi >= ki, s, -jnp.inf)
        m_prev = m_s[h]
        m_new = jnp.maximum(m_prev, jnp.max(s, axis=1, keepdims=True))
        p = jnp.concatenate(
            [jnp.exp2(s[:, c * LANES:(c + 1) * LANES] - m_new) for c in range(t // LANES)], axis=1)
        acc_s[h] = jnp.exp2(m_prev - m_new) * acc_s[h] + _dot(p.astype(BF16), v_ref[h])
        m_s[h] = m_new
        return carry

    lax.fori_loop(0, heads, head, 0, unroll=ATTN_HEAD_UNROLL)


def _attn_body(x_ref, sh_ref, sc_ref, g1_ref, wq_ref, wo_ref, k_ref, v_ref, cfp_ref, lng_ref, lnb_ref,
               o_ref, q_s, m_s, acc_s, o_s, *, heads, t):
    i = pl.program_id(1)
    j = pl.program_id(2)
    hd = ATTN_HEAD_DIM

    @pl.when(j == 0)
    def _():
        h = (x_ref[...] * (1.0 + sc_ref[...]) + sh_ref[...]).astype(BF16)
        qf = _dot(h, wq_ref[...]) * (hd ** -0.5 * LOG2E)
        extra = _place([cfp_ref[n] for n in range(3)], hd, 1.0)
        lane = lax.broadcasted_iota(jnp.int32, (t, AUG_LANES), 1)
        ones_q = jnp.where((lane >= hd + 3) & (lane < hd + 6), 1.0, 0.0)
        for hh in range(heads):
            ext = extra[:, hh * AUG_LANES:(hh + 1) * AUG_LANES] + ones_q
            q_s[hh] = jnp.where(lane < hd, _head_window(qf, hh, hd), ext).astype(BF16)
        m_s[...] = jnp.full(m_s.shape, -jnp.inf, F32)
        acc_s[...] = jnp.zeros(acc_s.shape, F32)

    block = functools.partial(_attn_block, q_s, k_ref, v_ref, m_s, acc_s, heads=heads, t=t)
    pl.when(j < i)(functools.partial(block, diagonal=False))
    pl.when(j == i)(functools.partial(block, diagonal=True))

    @pl.when(j == pl.num_programs(2) - 1)
    def _():
        for hh in range(heads):
            a = acc_s[hh]
            o_s[:, hh * hd:(hh + 1) * hd] = (a / a[:, hd:hd + 1])[:, :hd].astype(BF16)
        y = _dot(o_s[...], wo_ref[...])
        r_sum = DN_ALPHA * x_ref[...] + (1.0 + g1_ref[...]) * y
        o_ref[...] = _layer_norm(r_sum, lng_ref[...], lnb_ref[...])


def _attention(x, shift, scale, gate1, w_q, w_o, k_aug, v_aug, cf_parts, ln_g, ln_b):
    bsz, seq, d = x.shape
    heads = k_aug.shape[1]
    t = min(ATTN_ROWS, seq)
    vec = pl.BlockSpec((None, 1, d), lambda b, i, j: (b, 0, 0))
    full = lambda a: pl.BlockSpec(a.shape, lambda b, i, j: (0,) * a.ndim)
    kv_spec = pl.BlockSpec((None, heads, t, AUG_LANES), lambda b, i, j: (b, 0, jnp.minimum(j, i), 0))
    w_q, w_o, ln_g, ln_b = w_q.astype(BF16), w_o.astype(BF16), ln_g[None], ln_b[None]
    return pl.pallas_call(
        functools.partial(_attn_body, heads=heads, t=t),
        grid=(bsz, seq // t, seq // t),
        in_specs=[
            pl.BlockSpec((None, t, d), lambda b, i, j: (b, i, 0)),
            vec, vec, vec, full(w_q), full(w_o), kv_spec, kv_spec,
            pl.BlockSpec((None, 3, t, heads), lambda b, i, j: (b, 0, i, 0)),
            full(ln_g), full(ln_b),
        ],
        out_specs=pl.BlockSpec((None, t, d), lambda b, i, j: (b, i, 0)),
        out_shape=jax.ShapeDtypeStruct((bsz, seq, d), F32),
        scratch_shapes=[
            pltpu.VMEM((heads, t, AUG_LANES), BF16),
            pltpu.VMEM((heads, t, LANES), F32),
            pltpu.VMEM((heads, t, AUG_LANES), F32),
            pltpu.VMEM((t, d), BF16),
        ],
        compiler_params=_params("arbitrary", "arbitrary", "arbitrary"),
        name="fox_attention",
    )(x, shift, scale, gate1, w_q, w_o, k_aug, v_aug, cf_parts, ln_g, ln_b)


def _route(sel, scores):
    n_e, t = sel.shape
    per = n_e // N_EXPERT_GROUPS
    sub = lax.broadcasted_iota(jnp.int32, (per, t), 0)
    neg = -jnp.inf
    gs = jnp.zeros((N_EXPERT_GROUPS, t), F32)
    gidx = lax.broadcasted_iota(jnp.int32, (N_EXPERT_GROUPS, t), 0)
    for g in range(N_EXPERT_GROUPS):
        v = sel[g * per:(g + 1) * per, :]
        m1 = jnp.max(v, axis=0, keepdims=True)
        first = jnp.min(jnp.where(v == m1, sub, per), axis=0, keepdims=True)
        m2 = jnp.max(jnp.where(sub == first, neg, v), axis=0, keepdims=True)
        gs = jnp.where(gidx == g, m1 + m2, gs)
    grank = jnp.zeros((N_EXPERT_GROUPS, t), jnp.int32)
    for g in range(N_EXPERT_GROUPS):
        other = gs[g:g + 1, :]
        beats = (other > gs) | ((other >= gs) & (gidx > g))
        grank = grank + jnp.where(beats, 1, 0)
    masked = jnp.concatenate(
        [jnp.where(grank[g:g + 1, :] < TOPK_GROUPS, sel[g * per:(g + 1) * per, :], neg)
         for g in range(N_EXPERT_GROUPS)], axis=0)
    eidx = lax.broadcasted_iota(jnp.int32, (n_e, t), 0)
    erank = jnp.zeros((n_e, t), jnp.int32)
    for e in range(n_e):
        other = masked[e:e + 1, :]
        beats = (other > masked) | ((other >= masked) & (eidx > e))
        erank = erank + jnp.where(beats, 1, 0)
    w = jnp.where(erank < TOP_K, scores, 0.0)
    return w / jnp.sum(w, axis=0, keepdims=True) * ROUTED_SCALE


def _moe_body(x_ref, sh_ref, sc_ref, g2_ref, wr_ref, rb_ref, w1_ref, w3_ref, w2_ref, ws1_ref, ws3_ref, ws2_ref,
              lng_ref, lnb_ref, o_ref, h_s, gate_s, acc_s, *, eb, f_dim):
    j = pl.program_id(2)

    @pl.when(j == 0)
    def _():
        h = x_ref[...] * (1.0 + sc_ref[...]) + sh_ref[...]
        hb = h.astype(BF16)
        h_s[...] = hb
        scores = _sigmoid(_dot_nt(wr_ref[...], h, precision=HIGHEST))
        gate_s[...] = _route(scores + rb_ref[...], scores)
        a = _silu(_dot(hb, ws1_ref[...])) * _dot(hb, ws3_ref[...])
        acc_s[...] = _dot(a.astype(BF16), ws2_ref[...])

    hb = h_s[...]
    a = _silu(_dot(hb, w1_ref[...])) * _dot(hb, w3_ref[...])
    g = gate_s[pl.ds(pl.multiple_of(j * eb, eb), eb), :].astype(BF16)
    erow = lax.broadcasted_iota(jnp.int32, (eb, eb * f_dim), 0)
    ecol = lax.broadcasted_iota(jnp.int32, (eb, eb * f_dim), 1) // f_dim
    expand = (erow == ecol).astype(BF16)
    gexp = _dot_tn(g, expand)
    acc_s[...] += _dot((a * gexp).astype(BF16), w2_ref[...])

    @pl.when(j == pl.num_programs(2) - 1)
    def _():
        r_sum = DN_ALPHA * x_ref[...] + (1.0 + g2_ref[...]) * acc_s[...]
        o_ref[...] = _layer_norm(r_sum, lng_ref[...], lnb_ref[...])


def _moe(x, shift, scale, gate2, w_router, router_bias, w1, w3, w2, ws1, ws3, ws2, ln_g, ln_b):
    bsz, seq, d = x.shape
    n_e, _, f_dim = w1.shape
    tm = min(MOE_ROWS, seq)
    eb = MOE_EXPERT_BLOCK
    w1c = w1.astype(BF16).transpose(1, 0, 2).reshape(d, n_e * f_dim)
    w3c = w3.astype(BF16).transpose(1, 0, 2).reshape(d, n_e * f_dim)
    w2c = w2.astype(BF16).reshape(n_e * f_dim, d)
    wr_t = w_router.T
    rb = router_bias[:, None]
    ws1, ws3, ws2 = ws1.astype(BF16), ws3.astype(BF16), ws2.astype(BF16)
    ln_g, ln_b = ln_g[None], ln_b[None]
    vec = pl.BlockSpec((None, 1, d), lambda b, i, j: (b, 0, 0))
    full = lambda a: pl.BlockSpec(a.shape, lambda b, i, j: (0,) * a.ndim)
    return pl.pallas_call(
        functools.partial(_moe_body, eb=eb, f_dim=f_dim),
        grid=(bsz, seq // tm, n_e // eb),
        in_specs=[
            pl.BlockSpec((None, tm, d), lambda b, i, j: (b, i, 0)),
            vec, vec, vec, full(wr_t), full(rb),
            pl.BlockSpec((d, eb * f_dim), lambda b, i, j: (0, j)),
            pl.BlockSpec((d, eb * f_dim), lambda b, i, j: (0, j)),
            pl.BlockSpec((eb * f_dim, d), lambda b, i, j: (j, 0)),
            full(ws1), full(ws3), full(ws2), full(ln_g), full(ln_b),
        ],
        out_specs=pl.BlockSpec((None, tm, d), lambda b, i, j: (b, i, 0)),
        out_shape=jax.ShapeDtypeStruct((bsz, seq, d), F32),
        scratch_shapes=[
            pltpu.VMEM((tm, d), BF16),
            pltpu.VMEM((n_e, tm), F32),
            pltpu.VMEM((tm, d), F32),
        ],
        compiler_params=_params("arbitrary", "arbitrary", "arbitrary"),
        name="moe",
    )(x, shift, scale, gate2, wr_t, rb, w1c, w3c, w2c, ws1, ws3, ws2, ln_g, ln_b)


def kernel(x, c, ada_w, ada_b, ln1_g, ln1_b, ln2_g, ln2_b, ssm_w_in, ssm_conv_w, ssm_conv_b, ssm_dt_bias,
           ssm_a_log, ssm_d, ssm_norm_w, ssm_w_out, kv_ada_w, kv_ada_b, kv_w, kv_fb, attn_w_q, attn_w_o,
           moe_w_router, moe_bias, moe_w1, moe_w3, moe_w2, moe_ws1, moe_ws3, moe_ws2):
    d = x.shape[-1]
    mods = _adaln(c, ada_w, ada_b)
    kv_mod = _adaln(c, kv_ada_w[None], kv_ada_b[None])[0]
    part = lambda m, n: m[:, None, n * d:(n + 1) * d]
    heads = ssm_dt_bias.shape[-1]
    d_inner = ssm_norm_w.shape[-1]
    conv_dim = ssm_conv_w.shape[-1]
    k_sh = v_sh = cf_parts = None
    for layer in range(DEPTH):
        shift1, scale1, gate1, shift2, scale2, gate2 = (part(mods[layer], n) for n in range(6))
        if layer < N_A_LAYERS:
            a = layer
            z, xbc, dt_raw = _ssm_in(x, shift1, scale1, ssm_w_in[a], d_inner=d_inner, conv_dim=conv_dim, heads=heads)
            x = _ssm_core(z, xbc, dt_raw, x, gate1, ssm_conv_w[a], ssm_conv_b[a], ssm_dt_bias[a], ssm_a_log[a],
                          ssm_d[a], ssm_norm_w[a], ssm_w_out[a], ln1_g[layer], ln1_b[layer])
        else:
            b = layer - N_A_LAYERS
            x = _attention(x, shift1, scale1, gate1, attn_w_q[b], attn_w_o[b], k_sh, v_sh, cf_parts,
                           ln1_g[layer], ln1_b[layer])
        x = _moe(x, shift2, scale2, gate2, moe_w_router[layer], moe_bias[layer], moe_w1[layer], moe_w3[layer],
                 moe_w2[layer], moe_ws1[layer], moe_ws3[layer], moe_ws2[layer], ln2_g[layer], ln2_b[layer])
        if layer == N_A_LAYERS - 1:
            k_sh, v_sh, cf_parts = _shared_kv(x, part(kv_mod, 0), part(kv_mod, 1), kv_w, kv_fb)
    return x
```

```python
import functools

import jax
import jax.numpy as jnp
from jax import lax
from jax.experimental import pallas as pl
from jax.experimental.pallas import tpu as pltpu

F32 = jnp.float32
BF16 = jnp.bfloat16
HIGHEST = lax.Precision.HIGHEST

DEPTH = 4
N_A_LAYERS = DEPTH // 2

SSM_HEAD_DIM = 64
SSM_GROUPS = 4
SSM_STATE = 128
SSM_CONV = 4

ATTN_HEAD_DIM = 64

N_EXPERTS = 64
TOP_K = 8
N_EXPERT_GROUPS = 8
TOPK_GROUPS = 4
ROUTED_SCALE = 2.5

DN_ALPHA = (2.0 * DEPTH) ** 0.25
LN_EPS = 1e-5
RMS_EPS = 1e-5
LOG2E = 1.4426950408889634

LANES = 128
SUBLANES = 8
VMEM_LIMIT = 56 * 1024 * 1024

BF16_TILE_ROWS = 2 * SUBLANES
SSD_CHUNK = 128
CONV_COLS = 256
PROJ_ROWS = 256
ATTN_ROWS = 512
ATTN_HEAD_UNROLL = 8
MOE_ROWS = 512
MOE_EXPERT_BLOCK = 16


def _sigmoid(v):
    return 1.0 / (1.0 + jnp.exp(-v))


def _silu(v):
    return v * _sigmoid(v)


def _layer_norm(r, g, b):
    mu = jnp.mean(r, axis=-1, keepdims=True)
    d = r - mu
    var = jnp.mean(d * d, axis=-1, keepdims=True)
    return d * lax.rsqrt(var + LN_EPS) * g + b


def _dot(a, b):
    return jnp.dot(a, b, preferred_element_type=F32)


def _dot_nt(a, b, precision=None):
    return lax.dot_general(a, b, (((1,), (1,)), ((), ())), preferred_element_type=F32, precision=precision)


def _dot_tn(a, b, precision=None):
    return lax.dot_general(a, b, (((0,), (0,)), ((), ())), preferred_element_type=F32, precision=precision)


def _params(*sem):
    return pltpu.CompilerParams(dimension_semantics=sem, vmem_limit_bytes=VMEM_LIMIT)


def _adaln_body(c_ref, w_ref, b_ref, o_ref):
    cond = _silu(c_ref[...])
    o_ref[...] = jnp.dot(cond, w_ref[...], precision=HIGHEST, preferred_element_type=F32) + b_ref[...]


def _adaln(c, w, b):
    nl, d, n = w.shape
    bsz = c.shape[0]
    tn = 1024
    return pl.pallas_call(
        _adaln_body,
        grid=(nl, n // tn),
        in_specs=[
            pl.BlockSpec((bsz, d), lambda l, j: (0, 0)),
            pl.BlockSpec((None, d, tn), lambda l, j: (l, 0, j)),
            pl.BlockSpec((None, 1, tn), lambda l, j: (l, 0, j)),
        ],
        out_specs=pl.BlockSpec((None, bsz, tn), lambda l, j: (l, 0, j)),
        out_shape=jax.ShapeDtypeStruct((nl, bsz, n), F32),
        compiler_params=_params("arbitrary", "arbitrary"),
        name="adaln",
    )(c, w, b.reshape(nl, 1, n))


def _ssm_in_body(x_ref, sh_ref, sc_ref, w_ref, z_ref, xbc_ref, dt_ref, *, d_inner, conv_dim, heads):
    h = (x_ref[...] * (1.0 + sc_ref[...]) + sh_ref[...]).astype(BF16)
    z_ref[...] = _dot(h, w_ref[:, 0:d_inner]).astype(BF16)
    xbc_ref[...] = _dot(h, w_ref[:, d_inner:d_inner + conv_dim]).astype(BF16)
    dt_ref[...] = _dot(h, w_ref[:, d_inner + conv_dim:])[:, :heads]


def _ssm_in(x, shift, scale, w_in, *, d_inner, conv_dim, heads):
    bsz, seq, d = x.shape
    tm = min(PROJ_ROWS, seq)
    n_in = d_inner + conv_dim + heads
    n_pad = -n_in % LANES
    w = jnp.pad(w_in.astype(BF16), ((0, 0), (0, n_pad)))
    vec = pl.BlockSpec((None, 1, d), lambda b, i: (b, 0, 0))
    return pl.pallas_call(
        functools.partial(_ssm_in_body, d_inner=d_inner, conv_dim=conv_dim, heads=heads),
        grid=(bsz, seq // tm),
        in_specs=[
            pl.BlockSpec((None, tm, d), lambda b, i: (b, i, 0)),
            vec, vec,
            pl.BlockSpec((d, n_in + n_pad), lambda b, i: (0, 0)),
        ],
        out_specs=[
            pl.BlockSpec((None, tm, d_inner), lambda b, i: (b, i, 0)),
            pl.BlockSpec((None, tm, conv_dim), lambda b, i: (b, i, 0)),
            pl.BlockSpec((None, tm, heads), lambda b, i: (b, i, 0)),
        ],
        out_shape=[
            jax.ShapeDtypeStruct((bsz, seq, d_inner), BF16),
            jax.ShapeDtypeStruct((bsz, seq, conv_dim), BF16),
            jax.ShapeDtypeStruct((bsz, seq, heads), F32),
        ],
        compiler_params=_params("arbitrary", "arbitrary"),
        name="ssm_in",
    )(x, shift, scale, w)


def _ssm_core_body(z_ref, xbc_ref, dt_ref, x_ref, g1_ref, cw_ref, cb_ref, dtb_ref, alog_ref, dskip_ref,
                   nw_ref, wout_ref, lng_ref, lnb_ref, o_ref, tail_s, state, ybuf, *, heads, d_inner):
    q = SSD_CHUNK
    p_dim, n_dim = SSM_HEAD_DIM, SSM_STATE
    gn = SSM_GROUPS * n_dim
    hpg = heads // SSM_GROUPS
    tail = BF16_TILE_ROWS

    @pl.when(pl.program_id(1) == 0)
    def _():
        tail_s[...] = jnp.zeros(tail_s.shape, BF16)
        state[...] = jnp.zeros(state.shape, F32)

    u_ext = jnp.concatenate([tail_s[...], xbc_ref[...]], axis=0)
    tail_s[...] = xbc_ref[q - tail:q, :]
    tr = lax.broadcasted_iota(jnp.int32, (q, tail + q), 0)
    tc = lax.broadcasted_iota(jnp.int32, (q, tail + q), 1)
    shifts = [jnp.where(tc == tr + (tail - (SSM_CONV - 1) + k), 1.0, 0.0).astype(BF16) for k in range(SSM_CONV - 1)]
    chunks = []
    for c0 in range(0, u_ext.shape[1], CONV_COLS):
        cs = slice(c0, c0 + CONV_COLS)
        acc = cb_ref[:, cs] + cw_ref[SSM_CONV - 1:SSM_CONV, cs] * xbc_ref[:, cs].astype(F32)
        for k in range(SSM_CONV - 1):
            acc = acc + cw_ref[k:k + 1, cs] * _dot(shifts[k], u_ext[:, cs])
        chunks.append(_silu(acc))
    act = jnp.concatenate(chunks, axis=1)

    dt = dt_ref[...] + dtb_ref[...]
    dt = jnp.maximum(dt, 0.0) + jnp.log1p(jnp.exp(-jnp.abs(dt)))
    d_a = dt * (-LOG2E * jnp.exp(alog_ref[...]))
    row = lax.broadcasted_iota(jnp.int32, (q, q), 0)
    col = lax.broadcasted_iota(jnp.int32, (q, q), 1)
    causal = row >= col
    acum = jnp.dot(causal.astype(F32), d_a, precision=HIGHEST, preferred_element_type=F32)
    acum_t = acum.T
    dt_t = dt.T
    er = lax.broadcasted_iota(jnp.int32, (heads, d_inner), 0)
    ec = lax.broadcasted_iota(jnp.int32, (heads, d_inner), 1) // p_dim
    expand = jnp.where(er == ec, 1.0, 0.0).astype(BF16)

    def per_channel(v):
        hi = v.astype(BF16)
        lo = (v - hi.astype(F32)).astype(BF16)
        return _dot(hi, expand) + _dot(lo, expand)

    e_acum_x = per_channel(jnp.exp2(acum))
    w_end_x = per_channel(jnp.exp2(acum[q - 1:q, :] - acum) * dt)
    e_last_x = e_acum_x[q - 1:q, :]
    gw = hpg * p_dim
    lane = lax.broadcasted_iota(jnp.int32, (q, LANES), 1)

    for g in range(SSM_GROUPS):
        gs = slice(g * gw, (g + 1) * gw)
        b_g = act[:, d_inner + g * n_dim:d_inner + (g + 1) * n_dim].astype(BF16)
        c_g = act[:, d_inner + gn + g * n_dim:d_inner + gn + (g + 1) * n_dim].astype(BF16)
        cb = _dot_nt(c_g, b_g)
        xs_g = act[:, gs]
        st = state[g]
        y_off = _dot(c_g, st.astype(BF16)) * e_acum_x[:, gs]
        xw = (xs_g * w_end_x[:, gs]).astype(BF16)
        state[g] = st * e_last_x[:, gs] + _dot_tn(b_g, xw)
        for pr in range(gw // LANES):
            ms = []
            for h in range(g * hpg + 2 * pr, g * hpg + 2 * pr + 2):
                seg = acum[:, h:h + 1] - acum_t[h:h + 1, :]
                ms.append(cb * jnp.exp2(jnp.where(causal, seg, -jnp.inf)) * dt_t[h:h + 1, :])
            pair = xs_g[:, pr * LANES:(pr + 1) * LANES]
            rhs = jnp.concatenate([jnp.where(lane < p_dim, pair, 0.0), jnp.where(lane >= p_dim, pair, 0.0)], axis=0)
            y = _dot(jnp.concatenate(ms, axis=1).astype(BF16), rhs.astype(BF16))
            cs = slice(g * gw + pr * LANES, g * gw + (pr + 1) * LANES)
            ybuf[:, cs] = y + y_off[:, pr * LANES:(pr + 1) * LANES] + dskip_ref[:, cs] * pair

    y = ybuf[...] * _silu(z_ref[...].astype(F32))
    y = y * lax.rsqrt(jnp.mean(y * y, axis=-1, keepdims=True) + RMS_EPS) * nw_ref[...]
    out = _dot(y.astype(BF16), wout_ref[...])
    r_sum = DN_ALPHA * x_ref[...] + (1.0 + g1_ref[...]) * out
    o_ref[...] = _layer_norm(r_sum, lng_ref[...], lnb_ref[...])


def _ssm_core(z, xbc, dt_raw, x, gate1, conv_w, conv_b, dt_bias, a_log, d_skip, norm_w, w_out, ln_g, ln_b):
    bsz, seq, d = x.shape
    d_inner = z.shape[-1]
    conv_dim = xbc.shape[-1]
    heads = dt_raw.shape[-1]
    q = SSD_CHUNK
    rows = lambda n: pl.BlockSpec((None, q, n), lambda b, i: (b, i, 0))
    full = lambda a: pl.BlockSpec(a.shape, lambda b, i: (0,) * a.ndim)
    conv_b, dt_bias, a_log = conv_b[None], dt_bias[None], a_log[None]
    norm_w, ln_g, ln_b = norm_w[None], ln_g[None], ln_b[None]
    w_out = w_out.astype(BF16)
    d_skip = jnp.repeat(d_skip, SSM_HEAD_DIM)[None]
    return pl.pallas_call(
        functools.partial(_ssm_core_body, heads=heads, d_inner=d_inner),
        grid=(bsz, seq // q),
        in_specs=[
            rows(d_inner), rows(conv_dim), rows(heads), rows(d),
            pl.BlockSpec((None, 1, d), lambda b, i: (b, 0, 0)),
            full(conv_w), full(conv_b), full(dt_bias), full(a_log), full(d_skip),
            full(norm_w), full(w_out), full(ln_g), full(ln_b),
        ],
        out_specs=rows(d),
        out_shape=jax.ShapeDtypeStruct((bsz, seq, d), F32),
        scratch_shapes=[
            pltpu.VMEM((BF16_TILE_ROWS, conv_dim), BF16),
            pltpu.VMEM((SSM_GROUPS, SSM_STATE, d_inner // SSM_GROUPS), F32),
            pltpu.VMEM((q, d_inner), F32),
        ],
        compiler_params=_params("arbitrary", "arbitrary"),
        name="ssm_core",
    )(z, xbc, dt_raw, x, gate1, conv_w, conv_b, dt_bias, a_log, d_skip, norm_w, w_out, ln_g, ln_b)


AUG_LANES = LANES


def _split3(v):
    hi = v.astype(BF16)
    r = v - hi.astype(F32)
    mid = r.astype(BF16)
    lo = (r - mid.astype(F32)).astype(BF16)
    return hi, mid, lo


def _place(parts, first_lane, sign):
    heads = parts[0].shape[1]
    r = lax.broadcasted_iota(jnp.int32, (heads, heads * AUG_LANES), 0)
    c = lax.broadcasted_iota(jnp.int32, (heads, heads * AUG_LANES), 1)
    out = None
    for k, part in enumerate(parts):
        mat = jnp.where(c == r * AUG_LANES + (first_lane + k), sign, 0.0).astype(BF16)
        t = _dot(part, mat)
        out = t if out is None else out + t
    return out


def _head_window(m, h, hd):
    base = (h * hd // LANES) * LANES
    w = m[:, base:base + LANES]
    shift = (h * hd) % LANES
    return pltpu.roll(w, LANES - shift, axis=1) if shift else w


def _kv_body(x_ref, sh_ref, sc_ref, w_ref, fb_ref, k_ref, v_ref, cfp_ref, carry, *, d, heads):
    hd = ATTN_HEAD_DIM

    @pl.when(pl.program_id(1) == 0)
    def _():
        carry[...] = jnp.zeros(carry.shape, F32)

    u = (x_ref[...] * (1.0 + sc_ref[...]) + sh_ref[...]).astype(BF16)
    kf = _dot(u, w_ref[:, 0:d])
    vf = _dot(u, w_ref[:, d:2 * d])
    f = _dot(u, w_ref[:, 2 * d:])[:, :heads] + fb_ref[...]
    log_f = jnp.minimum(f, 0.0) - jnp.log1p(jnp.exp(-jnp.abs(f)))
    lt = log_f.T
    tm = lt.shape[1]
    lane_t = lax.broadcasted_iota(jnp.int32, lt.shape, 1)
    step = 1
    while step < tm:
        lt = lt + jnp.where(lane_t >= step, pltpu.roll(lt, step, axis=1), 0.0)
        step *= 2
    lt = lt + carry[...]
    carry[...] = lt[:, tm - 1:tm]
    parts = _split3(lt.T * LOG2E)
    for n in range(3):
        cfp_ref[n] = parts[n]
    extra = _place(parts, hd + 3, -1.0)
    lane = lax.broadcasted_iota(jnp.int32, (tm, AUG_LANES), 1)
    ones_k = jnp.where((lane >= hd) & (lane < hd + 3), 1.0, 0.0)
    ones_v = jnp.where(lane == hd, 1.0, 0.0)
    for h in range(heads):
        ext = extra[:, h * AUG_LANES:(h + 1) * AUG_LANES] + ones_k
        k_ref[h] = jnp.where(lane < hd, _head_window(kf, h, hd), ext).astype(BF16)
        v_ref[h] = jnp.where(lane < hd, _head_window(vf, h, hd), ones_v).astype(BF16)


def _shared_kv(x, shift, scale, kv_w, kv_fb):
    bsz, seq, d = x.shape
    heads = kv_w.shape[1] - 2 * d
    tm = min(PROJ_ROWS, seq)
    n_pad = -kv_w.shape[1] % LANES
    w = jnp.pad(kv_w.astype(BF16), ((0, 0), (0, n_pad)))
    vec = pl.BlockSpec((None, 1, d), lambda b, i: (b, 0, 0))
    aug = pl.BlockSpec((None, heads, tm, AUG_LANES), lambda b, i: (b, 0, i, 0))
    return pl.pallas_call(
        functools.partial(_kv_body, d=d, heads=heads),
        grid=(bsz, seq // tm),
        in_specs=[pl.BlockSpec((None, tm, d), lambda b, i: (b, i, 0)), vec, vec,
                  pl.BlockSpec(w.shape, lambda b, i: (0, 0)),
                  pl.BlockSpec((1, heads), lambda b, i: (0, 0))],
        out_specs=[aug, aug, pl.BlockSpec((None, 3, tm, heads), lambda b, i: (b, 0, i, 0))],
        out_shape=[
            jax.ShapeDtypeStruct((bsz, heads, seq, AUG_LANES), BF16),
            jax.ShapeDtypeStruct((bsz, heads, seq, AUG_LANES), BF16),
            jax.ShapeDtypeStruct((bsz, 3, seq, heads), BF16),
        ],
        scratch_shapes=[pltpu.VMEM((heads, 1), F32)],
        compiler_params=_params("arbitrary", "arbitrary"),
        name="shared_kv",
    )(x, shift, scale, w, kv_fb[None])


def _attn_block(q_s, k_ref, v_ref, m_s, acc_s, *, heads, t, diagonal):
    def head(h, carry):
        s = _dot_nt(q_s[h], k_ref[h])
        if diagonal:
            qi = lax.broadcasted_iota(jnp.int32, (t, t), 0)
            ki = lax.broadcasted_iota(jnp.int32, (t, t), 1)
            s = jnp.where(qi >= ki, s, -jnp.inf)
        m_prev = m_s[h]
        m_new = jnp.maximum(m_prev, jnp.max(s, axis=1, keepdims=True))
        p = jnp.concatenate(
            [jnp.exp2(s[:, c * LANES:(c + 1) * LANES] - m_new) for c in range(t // LANES)], axis=1)
        acc_s[h] = jnp.exp2(m_prev - m_new) * acc_s[h] + _dot(p.astype(BF16), v_ref[h])
        m_s[h] = m_new
        return carry

    lax.fori_loop(0, heads, head, 0, unroll=ATTN_HEAD_UNROLL)


def _attn_body(x_ref, sh_ref, sc_ref, g1_ref, wq_ref, wo_ref, k_ref, v_ref, cfp_ref, lng_ref, lnb_ref,
               o_ref, q_s, m_s, acc_s, o_s, *, heads, t):
    i = pl.program_id(1)
    j = pl.program_id(2)
    hd = ATTN_HEAD_DIM

    @pl.when(j == 0)
    def _():
        h = (x_ref[...] * (1.0 + sc_ref[...]) + sh_ref[...]).astype(BF16)
        qf = _dot(h, wq_ref[...]) * (hd ** -0.5 * LOG2E)
        extra = _place([cfp_ref[n] for n in range(3)], hd, 1.0)
        lane = lax.broadcasted_iota(jnp.int32, (t, AUG_LANES), 1)
        ones_q = jnp.where((lane >= hd + 3) & (lane < hd + 6), 1.0, 0.0)
        for hh in range(heads):
            ext = extra[:, hh * AUG_LANES:(hh + 1) * AUG_LANES] + ones_q
            q_s[hh] = jnp.where(lane < hd, _head_window(qf, hh, hd), ext).astype(BF16)
        m_s[...] = jnp.full(m_s.shape, -jnp.inf, F32)
        acc_s[...] = jnp.zeros(acc_s.shape, F32)

    block = functools.partial(_attn_block, q_s, k_ref, v_ref, m_s, acc_s, heads=heads, t=t)
    pl.when(j < i)(functools.partial(block, diagonal=False))
    pl.when(j == i)(functools.partial(block, diagonal=True))

    @pl.when(j == pl.num_programs(2) - 1)
    def _():
        for hh in range(heads):
            a = acc_s[hh]
            o_s[:, hh * hd:(hh + 1) * hd] = (a / a[:, hd:hd + 1])[:, :hd].astype(BF16)
        y = _dot(o_s[...], wo_ref[...])
        r_sum = DN_ALPHA * x_ref[...] + (1.0 + g1_ref[...]) * y
        o_ref[...] = _layer_norm(r_sum, lng_ref[...], lnb_ref[...])


def _attention(x, shift, scale, gate1, w_q, w_o, k_aug, v_aug, cf_parts, ln_g, ln_b):
    bsz, seq, d = x.shape
    heads = k_aug.shape[1]
    t = min(ATTN_ROWS, seq)
    vec = pl.BlockSpec((None, 1, d), lambda b, i, j: (b, 0, 0))
    full = lambda a: pl.BlockSpec(a.shape, lambda b, i, j: (0,) * a.ndim)
    kv_spec = pl.BlockSpec((None, heads, t, AUG_LANES), lambda b, i, j: (b, 0, jnp.minimum(j, i), 0))
    w_q, w_o, ln_g, ln_b = w_q.astype(BF16), w_o.astype(BF16), ln_g[None], ln_b[None]
    return pl.pallas_call(
        functools.partial(_attn_body, heads=heads, t=t),
        grid=(bsz, seq // t, seq // t),
        in_specs=[
            pl.BlockSpec((None, t, d), lambda b, i, j: (b, i, 0)),
            vec, vec, vec, full(w_q), full(w_o), kv_spec, kv_spec,
            pl.BlockSpec((None, 3, t, heads), lambda b, i, j: (b, 0, i, 0)),
            full(ln_g), full(ln_b),
        ],
        out_specs=pl.BlockSpec((None, t, d), lambda b, i, j: (b, i, 0)),
        out_shape=jax.ShapeDtypeStruct((bsz, seq, d), F32),
        scratch_shapes=[
            pltpu.VMEM((heads, t, AUG_LANES), BF16),
            pltpu.VMEM((heads, t, LANES), F32),
            pltpu.VMEM((heads, t, AUG_LANES), F32),
            pltpu.VMEM((t, d), BF16),
        ],
        compiler_params=_params("arbitrary", "arbitrary", "arbitrary"),
        name="fox_attention",
    )(x, shift, scale, gate1, w_q, w_o, k_aug, v_aug, cf_parts, ln_g, ln_b)


def _route(sel, scores):
    n_e, t = sel.shape
    per = n_e // N_EXPERT_GROUPS
    sub = lax.broadcasted_iota(jnp.int32, (per, t), 0)
    neg = -jnp.inf
    gs = jnp.zeros((N_EXPERT_GROUPS, t), F32)
    gidx = lax.broadcasted_iota(jnp.int32, (N_EXPERT_GROUPS, t), 0)
    for g in range(N_EXPERT_GROUPS):
        v = sel[g * per:(g + 1) * per, :]
        m1 = jnp.max(v, axis=0, keepdims=True)
        first = jnp.min(jnp.where(v == m1, sub, per), axis=0, keepdims=True)
        m2 = jnp.max(jnp.where(sub == first, neg, v), axis=0, keepdims=True)
        gs = jnp.where(gidx == g, m1 + m2, gs)
    grank = jnp.zeros((N_EXPERT_GROUPS, t), jnp.int32)
    for g in range(N_EXPERT_GROUPS):
        other = gs[g:g + 1, :]
        beats = (other > gs) | ((other >= gs) & (gidx > g))
        grank = grank + jnp.where(beats, 1, 0)
    masked = jnp.concatenate(
        [jnp.where(grank[g:g + 1, :] < TOPK_GROUPS, sel[g * per:(g + 1) * per, :], neg)
         for g in range(N_EXPERT_GROUPS)], axis=0)
    eidx = lax.broadcasted_iota(jnp.int32, (n_e, t), 0)
    work = masked
    w = jnp.zeros((n_e, t), F32)
    for _ in range(TOP_K):
        top = jnp.max(work, axis=0, keepdims=True)
        first = jnp.min(jnp.where(work == top, eidx, n_e), axis=0, keepdims=True)
        pick = eidx == first
        w = jnp.where(pick, scores, w)
        work = jnp.where(pick, neg, work)
    return w / jnp.sum(w, axis=0, keepdims=True) * ROUTED_SCALE


def _moe_body(x_ref, sh_ref, sc_ref, g2_ref, wr_ref, rb_ref, w1_ref, w3_ref, w2_ref, ws1_ref, ws3_ref, ws2_ref,
              lng_ref, lnb_ref, o_ref, h_s, gate_s, acc_s, *, eb, f_dim):
    j = pl.program_id(2)

    @pl.when(j == 0)
    def _():
        h = x_ref[...] * (1.0 + sc_ref[...]) + sh_ref[...]
        hb = h.astype(BF16)
        h_s[...] = hb
        scores = _sigmoid(_dot_nt(wr_ref[...], h, precision=HIGHEST))
        gate_s[...] = _route(scores + rb_ref[...], scores)
        a = _silu(_dot(hb, ws1_ref[...])) * _dot(hb, ws3_ref[...])
        acc_s[...] = _dot(a.astype(BF16), ws2_ref[...])

    hb = h_s[...]
    a = _silu(_dot(hb, w1_ref[...])) * _dot(hb, w3_ref[...])
    g = gate_s[pl.ds(pl.multiple_of(j * eb, eb), eb), :].astype(BF16)
    erow = lax.broadcasted_iota(jnp.int32, (eb, eb * f_dim), 0)
    ecol = lax.broadcasted_iota(jnp.int32, (eb, eb * f_dim), 1) // f_dim
    expand = (erow == ecol).astype(BF16)
    gexp = _dot_tn(g, expand)
    acc_s[...] += _dot((a * gexp).astype(BF16), w2_ref[...])

    @pl.when(j == pl.num_programs(2) - 1)
    def _():
        r_sum = DN_ALPHA * x_ref[...] + (1.0 + g2_ref[...]) * acc_s[...]
        o_ref[...] = _layer_norm(r_sum, lng_ref[...], lnb_ref[...])


def _moe(x, shift, scale, gate2, w_router, router_bias, w1, w3, w2, ws1, ws3, ws2, ln_g, ln_b):
    bsz, seq, d = x.shape
    n_e, _, f_dim = w1.shape
    tm = min(MOE_ROWS, seq)
    eb = MOE_EXPERT_BLOCK
    w1c = w1.astype(BF16).transpose(1, 0, 2).reshape(d, n_e * f_dim)
    w3c = w3.astype(BF16).transpose(1, 0, 2).reshape(d, n_e * f_dim)
    w2c = w2.astype(BF16).reshape(n_e * f_dim, d)
    wr_t = w_router.T
    rb = router_bias[:, None]
    ws1, ws3, ws2 = ws1.astype(BF16), ws3.astype(BF16), ws2.astype(BF16)
    ln_g, ln_b = ln_g[None], ln_b[None]
    vec = pl.BlockSpec((None, 1, d), lambda b, i, j: (b, 0, 0))
    full = lambda a: pl.BlockSpec(a.shape, lambda b, i, j: (0,) * a.ndim)
    return pl.pallas_call(
        functools.partial(_moe_body, eb=eb, f_dim=f_dim),
        grid=(bsz, seq // tm, n_e // eb),
        in_specs=[
            pl.BlockSpec((None, tm, d), lambda b, i, j: (b, i, 0)),
            vec, vec, vec, full(wr_t), full(rb),
            pl.BlockSpec((d, eb * f_dim), lambda b, i, j: (0, j)),
            pl.BlockSpec((d, eb * f_dim), lambda b, i, j: (0, j)),
            pl.BlockSpec((eb * f_dim, d), lambda b, i, j: (j, 0)),
            full(ws1), full(ws3), full(ws2), full(ln_g), full(ln_b),
        ],
        out_specs=pl.BlockSpec((None, tm, d), lambda b, i, j: (b, i, 0)),
        out_shape=jax.ShapeDtypeStruct((bsz, seq, d), F32),
        scratch_shapes=[
            pltpu.VMEM((tm, d), BF16),
            pltpu.VMEM((n_e, tm), F32),
            pltpu.VMEM((tm, d), F32),
        ],
        compiler_params=_params("arbitrary", "arbitrary", "arbitrary"),
        name="moe",
    )(x, shift, scale, gate2, wr_t, rb, w1c, w3c, w2c, ws1, ws3, ws2, ln_g, ln_b)


def kernel(x, c, ada_w, ada_b, ln1_g, ln1_b, ln2_g, ln2_b, ssm_w_in, ssm_conv_w, ssm_conv_b, ssm_dt_bias,
           ssm_a_log, ssm_d, ssm_norm_w, ssm_w_out, kv_ada_w, kv_ada_b, kv_w, kv_fb, attn_w_q, attn_w_o,
           moe_w_router, moe_bias, moe_w1, moe_w3, moe_w2, moe_ws1, moe_ws3, moe_ws2):
    d = x.shape[-1]
    mods = _adaln(c, ada_w, ada_b)
    kv_mod = _adaln(c, kv_ada_w[None], kv_ada_b[None])[0]
    part = lambda m, n: m[:, None, n * d:(n + 1) * d]
    heads = ssm_dt_bias.shape[-1]
    d_inner = ssm_norm_w.shape[-1]
    conv_dim = ssm_conv_w.shape[-1]
    k_sh = v_sh = cf_parts = None
    for layer in range(DEPTH):
        shift1, scale1, gate1, shift2, scale2, gate2 = (part(mods[layer], n) for n in range(6))
        if layer < N_A_LAYERS:
            a = layer
            z, xbc, dt_raw = _ssm_in(x, shift1, scale1, ssm_w_in[a], d_inner=d_inner, conv_dim=conv_dim, heads=heads)
            x = _ssm_core(z, xbc, dt_raw, x, gate1, ssm_conv_w[a], ssm_conv_b[a], ssm_dt_bias[a], ssm_a_log[a],
                          ssm_d[a], ssm_norm_w[a], ssm_w_out[a], ln1_g[layer], ln1_b[layer])
        else:
            b = layer - N_A_LAYERS
            x = _attention(x, shift1, scale1, gate1, attn_w_q[b], attn_w_o[b], k_sh, v_sh, cf_parts,
                           ln1_g[layer], ln1_b[layer])
        x = _moe(x, shift2, scale2, gate2, moe_w_router[layer], moe_bias[layer], moe_w1[layer], moe_w3[layer],
                 moe_w2[layer], moe_ws1[layer], moe_ws3[layer], moe_ws2[layer], ln2_g[layer], ln2_b[layer])
        if layer == N_A_LAYERS - 1:
            k_sh, v_sh, cf_parts = _shared_kv(x, part(kv_mod, 0), part(kv_mod, 1), kv_w, kv_fb)
    return x
```

```python
import functools

import jax
import jax.numpy as jnp
from jax import lax
from jax.experimental import pallas as pl
from jax.experimental.pallas import tpu as pltpu
from jax.experimental.pallas import tpu_sc as plsc

F32 = jnp.float32
BF16 = jnp.bfloat16
HIGHEST = lax.Precision.HIGHEST

DEPTH = 4
N_A_LAYERS = DEPTH // 2

SSM_HEAD_DIM = 64
SSM_GROUPS = 4
SSM_STATE = 128
SSM_CONV = 4

ATTN_HEAD_DIM = 64

N_EXPERTS = 64
TOP_K = 8
N_EXPERT_GROUPS = 8
TOPK_GROUPS = 4
ROUTED_SCALE = 2.5

DN_ALPHA = (2.0 * DEPTH) ** 0.25
LN_EPS = 1e-5
RMS_EPS = 1e-5
LOG2E = 1.4426950408889634

LANES = 128
SUBLANES = 8
VMEM_LIMIT = 56 * 1024 * 1024

BF16_TILE_ROWS = 2 * SUBLANES
SSD_CHUNK = 128
CONV_COLS = 256
PROJ_ROWS = 256
ATTN_ROWS = 512
ATTN_HEAD_UNROLL = 8
MOE_ROWS = 512
MOE_SLOT_TILE = 512
MOE_COMBINE_ROWS = 512
MOE_SC_CHUNK = 64
N_STREAMS = 2


def _sigmoid(v):
    return 1.0 / (1.0 + jnp.exp(-v))


def _silu(v):
    return v * _sigmoid(v)


def _layer_norm(r, g, b):
    mu = jnp.mean(r, axis=-1, keepdims=True)
    d = r - mu
    var = jnp.mean(d * d, axis=-1, keepdims=True)
    return d * lax.rsqrt(var + LN_EPS) * g + b


def _dot(a, b):
    return jnp.dot(a, b, preferred_element_type=F32)


def _dot_nt(a, b, precision=None):
    return lax.dot_general(a, b, (((1,), (1,)), ((), ())), preferred_element_type=F32, precision=precision)


def _dot_tn(a, b, precision=None):
    return lax.dot_general(a, b, (((0,), (0,)), ((), ())), preferred_element_type=F32, precision=precision)


def _params(*sem):
    return pltpu.CompilerParams(dimension_semantics=sem, vmem_limit_bytes=VMEM_LIMIT)


def _adaln_body(c_ref, w_ref, b_ref, o_ref):
    cond = _silu(c_ref[...])
    o_ref[...] = jnp.dot(cond, w_ref[...], precision=HIGHEST, preferred_element_type=F32) + b_ref[...]


def _adaln(c, w, b):
    nl, d, n = w.shape
    bsz = c.shape[0]
    tn = 1024
    return pl.pallas_call(
        _adaln_body,
        grid=(nl, n // tn),
        in_specs=[
            pl.BlockSpec((bsz, d), lambda l, j: (0, 0)),
            pl.BlockSpec((None, d, tn), lambda l, j: (l, 0, j)),
            pl.BlockSpec((None, 1, tn), lambda l, j: (l, 0, j)),
        ],
        out_specs=pl.BlockSpec((None, bsz, tn), lambda l, j: (l, 0, j)),
        out_shape=jax.ShapeDtypeStruct((nl, bsz, n), F32),
        compiler_params=_params("arbitrary", "arbitrary"),
        name="adaln",
    )(c, w, b.reshape(nl, 1, n))


def _ssm_in_body(x_ref, sh_ref, sc_ref, w_ref, z_ref, xbc_ref, dt_ref, *, d_inner, conv_dim, heads):
    h = (x_ref[...] * (1.0 + sc_ref[...]) + sh_ref[...]).astype(BF16)
    z_ref[...] = _dot(h, w_ref[:, 0:d_inner]).astype(BF16)
    xbc_ref[...] = _dot(h, w_ref[:, d_inner:d_inner + conv_dim]).astype(BF16)
    dt_ref[...] = _dot(h, w_ref[:, d_inner + conv_dim:])[:, :heads]


def _ssm_in(x, shift, scale, w_in, *, d_inner, conv_dim, heads):
    bsz, seq, d = x.shape
    tm = min(PROJ_ROWS, seq)
    n_in = d_inner + conv_dim + heads
    n_pad = -n_in % LANES
    w = jnp.pad(w_in.astype(BF16), ((0, 0), (0, n_pad)))
    vec = pl.BlockSpec((None, 1, d), lambda b, i: (b, 0, 0))
    return pl.pallas_call(
        functools.partial(_ssm_in_body, d_inner=d_inner, conv_dim=conv_dim, heads=heads),
        grid=(bsz, seq // tm),
        in_specs=[
            pl.BlockSpec((None, tm, d), lambda b, i: (b, i, 0)),
            vec, vec,
            pl.BlockSpec((d, n_in + n_pad), lambda b, i: (0, 0)),
        ],
        out_specs=[
            pl.BlockSpec((None, tm, d_inner), lambda b, i: (b, i, 0)),
            pl.BlockSpec((None, tm, conv_dim), lambda b, i: (b, i, 0)),
            pl.BlockSpec((None, tm, heads), lambda b, i: (b, i, 0)),
        ],
        out_shape=[
            jax.ShapeDtypeStruct((bsz, seq, d_inner), BF16),
            jax.ShapeDtypeStruct((bsz, seq, conv_dim), BF16),
            jax.ShapeDtypeStruct((bsz, seq, heads), F32),
        ],
        compiler_params=_params("arbitrary", "arbitrary"),
        name="ssm_in",
    )(x, shift, scale, w)


def _ssm_core_body(z_ref, xbc_ref, dt_ref, x_ref, g1_ref, cw_ref, cb_ref, dtb_ref, alog_ref, dskip_ref,
                   nw_ref, wout_ref, lng_ref, lnb_ref, o_ref, tail_s, state, ybuf, *, heads, d_inner):
    q = SSD_CHUNK
    p_dim, n_dim = SSM_HEAD_DIM, SSM_STATE
    gn = SSM_GROUPS * n_dim
    hpg = heads // SSM_GROUPS
    tail = BF16_TILE_ROWS

    @pl.when(pl.program_id(1) == 0)
    def _():
        tail_s[...] = jnp.zeros(tail_s.shape, BF16)
        state[...] = jnp.zeros(state.shape, F32)

    u_ext = jnp.concatenate([tail_s[...], xbc_ref[...]], axis=0)
    tail_s[...] = xbc_ref[q - tail:q, :]
    tr = lax.broadcasted_iota(jnp.int32, (q, tail + q), 0)
    tc = lax.broadcasted_iota(jnp.int32, (q, tail + q), 1)
    shifts = [jnp.where(tc == tr + (tail - (SSM_CONV - 1) + k), 1.0, 0.0).astype(BF16) for k in range(SSM_CONV - 1)]
    chunks = []
    for c0 in range(0, u_ext.shape[1], CONV_COLS):
        cs = slice(c0, c0 + CONV_COLS)
        acc = cb_ref[:, cs] + cw_ref[SSM_CONV - 1:SSM_CONV, cs] * xbc_ref[:, cs].astype(F32)
        for k in range(SSM_CONV - 1):
            acc = acc + cw_ref[k:k + 1, cs] * _dot(shifts[k], u_ext[:, cs])
        chunks.append(_silu(acc))
    act = jnp.concatenate(chunks, axis=1)

    dt = dt_ref[...] + dtb_ref[...]
    dt = jnp.maximum(dt, 0.0) + jnp.log1p(jnp.exp(-jnp.abs(dt)))
    d_a = dt * (-LOG2E * jnp.exp(alog_ref[...]))
    row = lax.broadcasted_iota(jnp.int32, (q, q), 0)
    col = lax.broadcasted_iota(jnp.int32, (q, q), 1)
    causal = row >= col
    acum = jnp.dot(causal.astype(F32), d_a, precision=HIGHEST, preferred_element_type=F32)
    acum_t = acum.T
    dt_t = dt.T
    er = lax.broadcasted_iota(jnp.int32, (heads, d_inner), 0)
    ec = lax.broadcasted_iota(jnp.int32, (heads, d_inner), 1) // p_dim
    expand = jnp.where(er == ec, 1.0, 0.0).astype(BF16)

    def per_channel(v):
        hi = v.astype(BF16)
        lo = (v - hi.astype(F32)).astype(BF16)
        return _dot(hi, expand) + _dot(lo, expand)

    e_acum_x = per_channel(jnp.exp2(acum))
    w_end_x = per_channel(jnp.exp2(acum[q - 1:q, :] - acum) * dt)
    e_last_x = e_acum_x[q - 1:q, :]
    gw = hpg * p_dim
    lane = lax.broadcasted_iota(jnp.int32, (q, LANES), 1)

    for g in range(SSM_GROUPS):
        gs = slice(g * gw, (g + 1) * gw)
        b_g = act[:, d_inner + g * n_dim:d_inner + (g + 1) * n_dim].astype(BF16)
        c_g = act[:, d_inner + gn + g * n_dim:d_inner + gn + (g + 1) * n_dim].astype(BF16)
        cb = _dot_nt(c_g, b_g)
        xs_g = act[:, gs]
        st = state[g]
        y_off = _dot(c_g, st.astype(BF16)) * e_acum_x[:, gs]
        xw = (xs_g * w_end_x[:, gs]).astype(BF16)
        state[g] = st * e_last_x[:, gs] + _dot_tn(b_g, xw)
        for pr in range(gw // LANES):
            ms = []
            for h in range(g * hpg + 2 * pr, g * hpg + 2 * pr + 2):
                seg = acum[:, h:h + 1] - acum_t[h:h + 1, :]
                ms.append(cb * jnp.exp2(jnp.where(causal, seg, -jnp.inf)) * dt_t[h:h + 1, :])
            pair = xs_g[:, pr * LANES:(pr + 1) * LANES]
            rhs = jnp.concatenate([jnp.where(lane < p_dim, pair, 0.0), jnp.where(lane >= p_dim, pair, 0.0)], axis=0)
            y = _dot(jnp.concatenate(ms, axis=1).astype(BF16), rhs.astype(BF16))
            cs = slice(g * gw + pr * LANES, g * gw + (pr + 1) * LANES)
            ybuf[:, cs] = y + y_off[:, pr * LANES:(pr + 1) * LANES] + dskip_ref[:, cs] * pair

    y = ybuf[...] * _silu(z_ref[...].astype(F32))
    y = y * lax.rsqrt(jnp.mean(y * y, axis=-1, keepdims=True) + RMS_EPS) * nw_ref[...]
    out = _dot(y.astype(BF16), wout_ref[...])
    r_sum = DN_ALPHA * x_ref[...] + (1.0 + g1_ref[...]) * out
    o_ref[...] = _layer_norm(r_sum, lng_ref[...], lnb_ref[...])


def _ssm_core(z, xbc, dt_raw, x, gate1, conv_w, conv_b, dt_bias, a_log, d_skip, norm_w, w_out, ln_g, ln_b):
    bsz, seq, d = x.shape
    d_inner = z.shape[-1]
    conv_dim = xbc.shape[-1]
    heads = dt_raw.shape[-1]
    q = SSD_CHUNK
    rows = lambda n: pl.BlockSpec((None, q, n), lambda b, i: (b, i, 0))
    full = lambda a: pl.BlockSpec(a.shape, lambda b, i: (0,) * a.ndim)
    conv_b, dt_bias, a_log = conv_b[None], dt_bias[None], a_log[None]
    norm_w, ln_g, ln_b = norm_w[None], ln_g[None], ln_b[None]
    w_out = w_out.astype(BF16)
    d_skip = jnp.repeat(d_skip, SSM_HEAD_DIM)[None]
    return pl.pallas_call(
        functools.partial(_ssm_core_body, heads=heads, d_inner=d_inner),
        grid=(bsz, seq // q),
        in_specs=[
            rows(d_inner), rows(conv_dim), rows(heads), rows(d),
            pl.BlockSpec((None, 1, d), lambda b, i: (b, 0, 0)),
            full(conv_w), full(conv_b), full(dt_bias), full(a_log), full(d_skip),
            full(norm_w), full(w_out), full(ln_g), full(ln_b),
        ],
        out_specs=rows(d),
        out_shape=jax.ShapeDtypeStruct((bsz, seq, d), F32),
        scratch_shapes=[
            pltpu.VMEM((BF16_TILE_ROWS, conv_dim), BF16),
            pltpu.VMEM((SSM_GROUPS, SSM_STATE, d_inner // SSM_GROUPS), F32),
            pltpu.VMEM((q, d_inner), F32),
        ],
        compiler_params=_params("arbitrary", "arbitrary"),
        name="ssm_core",
    )(z, xbc, dt_raw, x, gate1, conv_w, conv_b, dt_bias, a_log, d_skip, norm_w, w_out, ln_g, ln_b)


AUG_LANES = LANES


def _split3(v):
    hi = v.astype(BF16)
    r = v - hi.astype(F32)
    mid = r.astype(BF16)
    lo = (r - mid.astype(F32)).astype(BF16)
    return hi, mid, lo


def _place(parts, first_lane, sign):
    heads = parts[0].shape[1]
    r = lax.broadcasted_iota(jnp.int32, (heads, heads * AUG_LANES), 0)
    c = lax.broadcasted_iota(jnp.int32, (heads, heads * AUG_LANES), 1)
    out = None
    for k, part in enumerate(parts):
        mat = jnp.where(c == r * AUG_LANES + (first_lane + k), sign, 0.0).astype(BF16)
        t = _dot(part, mat)
        out = t if out is None else out + t
    return out


def _head_window(m, h, hd):
    base = (h * hd // LANES) * LANES
    w = m[:, base:base + LANES]
    shift = (h * hd) % LANES
    return pltpu.roll(w, LANES - shift, axis=1) if shift else w


def _kv_body(x_ref, sh_ref, sc_ref, w_ref, fb_ref, k_ref, v_ref, cfp_ref, carry, *, d, heads):
    hd = ATTN_HEAD_DIM

    @pl.when(pl.program_id(1) == 0)
    def _():
        carry[...] = jnp.zeros(carry.shape, F32)

    u = (x_ref[...] * (1.0 + sc_ref[...]) + sh_ref[...]).astype(BF16)
    kf = _dot(u, w_ref[:, 0:d])
    vf = _dot(u, w_ref[:, d:2 * d])
    f = _dot(u, w_ref[:, 2 * d:])[:, :heads] + fb_ref[...]
    log_f = jnp.minimum(f, 0.0) - jnp.log1p(jnp.exp(-jnp.abs(f)))
    lt = log_f.T
    tm = lt.shape[1]
    lane_t = lax.broadcasted_iota(jnp.int32, lt.shape, 1)
    step = 1
    while step < tm:
        lt = lt + jnp.where(lane_t >= step, pltpu.roll(lt, step, axis=1), 0.0)
        step *= 2
    lt = lt + carry[...]
    carry[...] = lt[:, tm - 1:tm]
    parts = _split3(lt.T * LOG2E)
    for n in range(3):
        cfp_ref[n] = parts[n]
    extra = _place(parts, hd + 3, -1.0)
    lane = lax.broadcasted_iota(jnp.int32, (tm, AUG_LANES), 1)
    ones_k = jnp.where((lane >= hd) & (lane < hd + 3), 1.0, 0.0)
    ones_v = jnp.where(lane == hd, 1.0, 0.0)
    for h in range(heads):
        ext = extra[:, h * AUG_LANES:(h + 1) * AUG_LANES] + ones_k
        k_ref[h] = jnp.where(lane < hd, _head_window(kf, h, hd), ext).astype(BF16)
        v_ref[h] = jnp.where(lane < hd, _head_window(vf, h, hd), ones_v).astype(BF16)


def _shared_kv(x, shift, scale, kv_w, kv_fb):
    bsz, seq, d = x.shape
    heads = kv_w.shape[1] - 2 * d
    tm = min(PROJ_ROWS, seq)
    n_pad = -kv_w.shape[1] % LANES
    w = jnp.pad(kv_w.astype(BF16), ((0, 0), (0, n_pad)))
    vec = pl.BlockSpec((None, 1, d), lambda b, i: (b, 0, 0))
    aug = pl.BlockSpec((None, heads, tm, AUG_LANES), lambda b, i: (b, 0, i, 0))
    return pl.pallas_call(
        functools.partial(_kv_body, d=d, heads=heads),
        grid=(bsz, seq // tm),
        in_specs=[pl.BlockSpec((None, tm, d), lambda b, i: (b, i, 0)), vec, vec,
                  pl.BlockSpec(w.shape, lambda b, i: (0, 0)),
                  pl.BlockSpec((1, heads), lambda b, i: (0, 0))],
        out_specs=[aug, aug, pl.BlockSpec((None, 3, tm, heads), lambda b, i: (b, 0, i, 0))],
        out_shape=[
            jax.ShapeDtypeStruct((bsz, heads, seq, AUG_LANES), BF16),
            jax.ShapeDtypeStruct((bsz, heads, seq, AUG_LANES), BF16),
            jax.ShapeDtypeStruct((bsz, 3, seq, heads), BF16),
        ],
        scratch_shapes=[pltpu.VMEM((heads, 1), F32)],
        compiler_params=_params("arbitrary", "arbitrary"),
        name="shared_kv",
    )(x, shift, scale, w, kv_fb[None])


def _attn_block(q_s, k_ref, v_ref, m_s, acc_s, *, heads, t, diagonal):
    def head(h, carry):
        s = _dot_nt(q_s[h], k_ref[h])
        if diagonal:
            qi = lax.broadcasted_iota(jnp.int32, (t, t), 0)
            ki = lax.broadcasted_iota(jnp.int32, (t, t), 1)
            s = jnp.where(qi >= ki, s, -jnp.inf)
        m_prev = m_s[h]
        m_new = jnp.maximum(m_prev, jnp.max(s, axis=1, keepdims=True))
        p = jnp.concatenate(
            [jnp.exp2(s[:, c * LANES:(c + 1) * LANES] - m_new) for c in range(t // LANES)], axis=1)
        acc_s[h] = jnp.exp2(m_prev - m_new) * acc_s[h] + _dot(p.astype(BF16), v_ref[h])
        m_s[h] = m_new
        return carry

    lax.fori_loop(0, heads, head, 0, unroll=ATTN_HEAD_UNROLL)


def _attn_body(x_ref, sh_ref, sc_ref, g1_ref, wq_ref, wo_ref, k_ref, v_ref, cfp_ref, lng_ref, lnb_ref,
               o_ref, q_s, m_s, acc_s, o_s, *, heads, t):
    i = pl.program_id(1)
    j = pl.program_id(2)
    hd = ATTN_HEAD_DIM

    @pl.when(j == 0)
    def _():
        h = (x_ref[...] * (1.0 + sc_ref[...]) + sh_ref[...]).astype(BF16)
        qf = _dot(h, wq_ref[...]) * (hd ** -0.5 * LOG2E)
        extra = _place([cfp_ref[n] for n in range(3)], hd, 1.0)
        lane = lax.broadcasted_iota(jnp.int32, (t, AUG_LANES), 1)
        ones_q = jnp.where((lane >= hd + 3) & (lane < hd + 6), 1.0, 0.0)
        for hh in range(heads):
            ext = extra[:, hh * AUG_LANES:(hh + 1) * AUG_LANES] + ones_q
            q_s[hh] = jnp.where(lane < hd, _head_window(qf, hh, hd), ext).astype(BF16)
        m_s[...] = jnp.full(m_s.shape, -jnp.inf, F32)
        acc_s[...] = jnp.zeros(acc_s.shape, F32)

    block = functools.partial(_attn_block, q_s, k_ref, v_ref, m_s, acc_s, heads=heads, t=t)
    pl.when(j < i)(functools.partial(block, diagonal=False))
    pl.when(j == i)(functools.partial(block, diagonal=True))

    @pl.when(j == pl.num_programs(2) - 1)
    def _():
        for hh in range(heads):
            a = acc_s[hh]
            o_s[:, hh * hd:(hh + 1) * hd] = (a / a[:, hd:hd + 1])[:, :hd].astype(BF16)
        y = _dot(o_s[...], wo_ref[...])
        r_sum = DN_ALPHA * x_ref[...] + (1.0 + g1_ref[...]) * y
        o_ref[...] = _layer_norm(r_sum, lng_ref[...], lnb_ref[...])


def _attention(x, shift, scale, gate1, w_q, w_o, k_aug, v_aug, cf_parts, ln_g, ln_b):
    bsz, seq, d = x.shape
    heads = k_aug.shape[1]
    t = min(ATTN_ROWS, seq)
    vec = pl.BlockSpec((None, 1, d), lambda b, i, j: (b, 0, 0))
    full = lambda a: pl.BlockSpec(a.shape, lambda b, i, j: (0,) * a.ndim)
    kv_spec = pl.BlockSpec((None, heads, t, AUG_LANES), lambda b, i, j: (b, 0, jnp.minimum(j, i), 0))
    w_q, w_o, ln_g, ln_b = w_q.astype(BF16), w_o.astype(BF16), ln_g[None], ln_b[None]
    return pl.pallas_call(
        functools.partial(_attn_body, heads=heads, t=t),
        grid=(bsz, seq // t, seq // t),
        in_specs=[
            pl.BlockSpec((None, t, d), lambda b, i, j: (b, i, 0)),
            vec, vec, vec, full(w_q), full(w_o), kv_spec, kv_spec,
            pl.BlockSpec((None, 3, t, heads), lambda b, i, j: (b, 0, i, 0)),
            full(ln_g), full(ln_b),
        ],
        out_specs=pl.BlockSpec((None, t, d), lambda b, i, j: (b, i, 0)),
        out_shape=jax.ShapeDtypeStruct((bsz, seq, d), F32),
        scratch_shapes=[
            pltpu.VMEM((heads, t, AUG_LANES), BF16),
            pltpu.VMEM((heads, t, LANES), F32),
            pltpu.VMEM((heads, t, AUG_LANES), F32),
            pltpu.VMEM((t, d), BF16),
        ],
        compiler_params=_params("arbitrary", "arbitrary", "arbitrary"),
        name="fox_attention",
    )(x, shift, scale, gate1, w_q, w_o, k_aug, v_aug, cf_parts, ln_g, ln_b)


def _route(sel, scores):
    n_e, t = sel.shape
    per = n_e // N_EXPERT_GROUPS
    sub = lax.broadcasted_iota(jnp.int32, (per, t), 0)
    neg = -jnp.inf
    gs = jnp.zeros((N_EXPERT_GROUPS, t), F32)
    gidx = lax.broadcasted_iota(jnp.int32, (N_EXPERT_GROUPS, t), 0)
    for g in range(N_EXPERT_GROUPS):
        v = sel[g * per:(g + 1) * per, :]
        m1 = jnp.max(v, axis=0, keepdims=True)
        first = jnp.min(jnp.where(v == m1, sub, per), axis=0, keepdims=True)
        m2 = jnp.max(jnp.where(sub == first, neg, v), axis=0, keepdims=True)
        gs = jnp.where(gidx == g, m1 + m2, gs)
    grank = jnp.zeros((N_EXPERT_GROUPS, t), jnp.int32)
    for g in range(N_EXPERT_GROUPS):
        other = gs[g:g + 1, :]
        beats = (other > gs) | ((other >= gs) & (gidx > g))
        grank = grank + jnp.where(beats, 1, 0)
    masked = jnp.concatenate(
        [jnp.where(grank[g:g + 1, :] < TOPK_GROUPS, sel[g * per:(g + 1) * per, :], neg)
         for g in range(N_EXPERT_GROUPS)], axis=0)
    eidx = lax.broadcasted_iota(jnp.int32, (n_e, t), 0)
    work = masked
    w = jnp.zeros((n_e, t), F32)
    chosen = jnp.zeros((n_e, t), F32)
    for _ in range(TOP_K):
        top = jnp.max(work, axis=0, keepdims=True)
        first = jnp.min(jnp.where(work == top, eidx, n_e), axis=0, keepdims=True)
        pick = eidx == first
        w = jnp.where(pick, scores, w)
        chosen = jnp.where(pick, 1.0, chosen)
        work = jnp.where(pick, neg, work)
    return w / jnp.sum(w, axis=0, keepdims=True) * ROUTED_SCALE, chosen


def _pack_halves(v):
    half = v.shape[1] // 2
    bits = lambda a: lax.bitcast_convert_type(a.astype(BF16).astype(F32), jnp.uint32)
    word = (bits(v[:, half:]) & jnp.uint32(0xFFFF0000)) | (bits(v[:, :half]) >> 16)
    return lax.bitcast_convert_type(word, jnp.int32)


def _unpack_halves(w):
    u = lax.bitcast_convert_type(w, jnp.uint32)
    lo = lax.bitcast_convert_type(u << 16, F32)
    hi = lax.bitcast_convert_type(u & jnp.uint32(0xFFFF0000), F32)
    return lo, hi


def _dot_halves(lo, hi, w_ref):
    half = lo.shape[1]
    return _dot(lo.astype(BF16), w_ref[:half, :]) + _dot(hi.astype(BF16), w_ref[half:, :])


def _route_body(x_ref, sh_ref, sc_ref, wr_ref, rb_ref, hp_ref, e8_ref, p8_ref, g8_ref, cnt_ref, carry):
    @pl.when((pl.program_id(0) == 0) & (pl.program_id(1) == 0))
    def _():
        carry[...] = jnp.zeros(carry.shape, F32)

    h = x_ref[...] * (1.0 + sc_ref[...]) + sh_ref[...]
    hp_ref[...] = _pack_halves(h)
    scores = _sigmoid(_dot_nt(wr_ref[...], h, precision=HIGHEST))
    gate, chosen = _route(scores + rb_ref[...], scores)
    n_e, tm = chosen.shape
    lane = lax.broadcasted_iota(jnp.int32, (n_e, tm), 1)
    incl = chosen
    step = 1
    while step < tm:
        incl = incl + jnp.where(lane >= step, pltpu.roll(incl, step, axis=1), 0.0)
        step *= 2
    pos = carry[...] + incl - chosen
    carry[...] = carry[...] + incl[:, tm - 1:tm]
    cnt_ref[...] = jnp.broadcast_to(carry[...], cnt_ref.shape)
    er = lax.broadcasted_iota(jnp.int32, (n_e, n_e), 0)
    ec = lax.broadcasted_iota(jnp.int32, (n_e, n_e), 1)
    before = jnp.where(ec < er, 1.0, 0.0).astype(BF16)
    rank = _dot(before, chosen.astype(BF16))
    eidx = lax.broadcasted_iota(jnp.int32, (n_e, tm), 0).astype(F32)
    row = lax.broadcasted_iota(jnp.int32, (TOP_K, tm), 0)
    e8 = jnp.zeros((TOP_K, tm), F32)
    p8 = jnp.zeros((TOP_K, tm), F32)
    g8 = jnp.zeros((TOP_K, tm), F32)
    for k in range(TOP_K):
        sel = jnp.where((rank == k) & (chosen > 0.0), 1.0, 0.0)
        e8 = jnp.where(row == k, jnp.sum(sel * eidx, axis=0, keepdims=True), e8)
        p8 = jnp.where(row == k, jnp.sum(sel * pos, axis=0, keepdims=True), p8)
        g8 = jnp.where(row == k, jnp.sum(sel * gate, axis=0, keepdims=True), g8)
    e8_ref[...] = e8.astype(jnp.int32)
    p8_ref[...] = p8.astype(jnp.int32)
    g8_ref[...] = g8


def _moe_route(x, shift, scale, w_router, router_bias):
    bsz, seq, d = x.shape
    n_e = w_router.shape[1]
    tm = min(MOE_ROWS, seq)
    nt = seq // tm
    tokens = bsz * seq
    vec = pl.BlockSpec((None, 1, d), lambda b, i: (b, 0, 0))
    per_tok = pl.BlockSpec((TOP_K, tm), lambda b, i: (0, b * nt + i))
    wr_t = w_router.T
    rb = router_bias[:, None]
    return pl.pallas_call(
        _route_body,
        grid=(bsz, nt),
        in_specs=[pl.BlockSpec((None, tm, d), lambda b, i: (b, i, 0)), vec, vec,
                  pl.BlockSpec(wr_t.shape, lambda b, i: (0, 0)), pl.BlockSpec(rb.shape, lambda b, i: (0, 0))],
        out_specs=[pl.BlockSpec((tm, d // 2), lambda b, i: (b * nt + i, 0)), per_tok, per_tok, per_tok,
                   pl.BlockSpec((n_e, LANES), lambda b, i: (0, 0))],
        out_shape=[
            jax.ShapeDtypeStruct((tokens, d // 2), jnp.int32),
            jax.ShapeDtypeStruct((TOP_K, tokens), jnp.int32),
            jax.ShapeDtypeStruct((TOP_K, tokens), jnp.int32),
            jax.ShapeDtypeStruct((TOP_K, tokens), F32),
            jax.ShapeDtypeStruct((n_e, LANES), F32),
        ],
        scratch_shapes=[pltpu.VMEM((n_e, 1), F32)],
        compiler_params=_params("arbitrary", "arbitrary"),
        name="moe_route",
    )(x, shift, scale, wr_t, rb)


def _dest_body(base_ref, e8_ref, p8_ref, o_ref, *, n_e):
    e8 = e8_ref[...]
    dest = p8_ref[...]
    for e in range(n_e):
        dest = dest + jnp.where(e8 == e, base_ref[e], 0)
    o_ref[...] = dest


def _moe_dest(base, e8, p8):
    tokens = e8.shape[1]
    tb = min(2048, tokens)
    blk = pl.BlockSpec((TOP_K, tb), lambda i, base: (0, i))
    return pl.pallas_call(
        functools.partial(_dest_body, n_e=base.shape[0]),
        grid_spec=pltpu.PrefetchScalarGridSpec(num_scalar_prefetch=1, grid=(tokens // tb,), in_specs=[blk, blk],
                                               out_specs=blk),
        out_shape=jax.ShapeDtypeStruct((TOP_K, tokens), jnp.int32),
        compiler_params=_params("arbitrary"),
        name="moe_dest",
    )(base, e8, p8)


def _sc_mesh():
    return plsc.VectorSubcoreMesh(core_axis_name="c", subcore_axis_name="s")


def _sc_dispatch(rows, dest, n_slots):
    tokens, width = rows.shape
    chunk = dest.shape[2]
    sc = plsc.get_sparse_core_info()
    n_cores, n_workers = sc.num_cores, sc.num_cores * sc.num_subcores
    t_per_w = tokens // n_workers
    n_chunks = t_per_w // chunk
    assert n_chunks * chunk * n_workers == tokens and n_chunks % 2 == 0

    def body(rows_hbm, dest_hbm, out_hbm, idx0, idx1, buf0, buf1, rs0, rs1, ws0, ws1):
        idx, bufs, rsem, wsem = (idx0, idx1), (buf0, buf1), (rs0, rs1), (ws0, ws1)
        wid = lax.axis_index("s") * n_cores + lax.axis_index("c")

        def read(i, b):
            return pltpu.make_async_copy(rows_hbm.at[pl.ds(wid * t_per_w + i * chunk, chunk)], bufs[b], rsem[b])

        def scatter(b, k):
            return pltpu.make_async_copy(bufs[b], out_hbm.at[idx[b].at[k]], wsem[b])

        read(0, 0).start()

        def pair(g, carry):
            for b in range(2):
                i = g * 2 + b

                @pl.when(i + 1 < n_chunks)
                def _():
                    @pl.when(i >= 1)
                    def _():
                        for k in range(TOP_K):
                            scatter(1 - b, k).wait()
                    read(i + 1, 1 - b).start()

                pltpu.sync_copy(dest_hbm.at[wid * n_chunks + i], idx[b])
                read(i, b).wait()
                for k in range(TOP_K):
                    scatter(b, k).start()
            return carry

        lax.fori_loop(0, n_chunks // 2, pair, 0)
        for b in range(2):
            for k in range(TOP_K):
                scatter(b, k).wait()

    return pl.kernel(
        body, mesh=_sc_mesh(), out_type=jax.ShapeDtypeStruct((n_slots, width), jnp.int32),
        scratch_types=[pltpu.VMEM((TOP_K, chunk), jnp.int32)] * 2 + [pltpu.VMEM((chunk, width), jnp.int32)] * 2
        + [pltpu.SemaphoreType.DMA] * 4,
    )(rows, dest)


def _sc_return(ys, dest):
    width = ys.shape[1]
    n_all, _, chunk = dest.shape
    tokens = n_all * chunk
    sc = plsc.get_sparse_core_info()
    n_cores, n_workers = sc.num_cores, sc.num_cores * sc.num_subcores
    t_per_w = tokens // n_workers
    n_chunks = t_per_w // chunk
    n_items = n_chunks * TOP_K
    assert n_chunks * chunk * n_workers == tokens

    def body(ys_hbm, dest_hbm, out_hbm, idx, buf0, buf1, gs0, gs1, ws0, ws1):
        bufs, gsem, wsem = (buf0, buf1), (gs0, gs1), (ws0, ws1)
        wid = lax.axis_index("s") * n_cores + lax.axis_index("c")
        pltpu.sync_copy(dest_hbm.at[pl.ds(wid * n_chunks, n_chunks)], idx)

        def gather(it, b):
            return pltpu.make_async_copy(ys_hbm.at[idx.at[it // TOP_K, it % TOP_K]], bufs[b], gsem[b])

        def write(it, b):
            dst = out_hbm.at[it % TOP_K, pl.ds(wid * t_per_w + (it // TOP_K) * chunk, chunk)]
            return pltpu.make_async_copy(bufs[b], dst, wsem[b])

        gather(0, 0).start()

        def pair(g, carry):
            for b in range(2):
                it = g * 2 + b

                @pl.when(it + 1 < n_items)
                def _():
                    @pl.when(it >= 1)
                    def _():
                        write(it - 1, 1 - b).wait()
                    gather(it + 1, 1 - b).start()

                gather(it, b).wait()
                write(it, b).start()
            return carry

        lax.fori_loop(0, n_items // 2, pair, 0)
        write(n_items - 2, 0).wait()
        write(n_items - 1, 1).wait()

    return pl.kernel(
        body, mesh=_sc_mesh(), out_type=jax.ShapeDtypeStruct((TOP_K, tokens, width), jnp.int32),
        scratch_types=[pltpu.VMEM((n_chunks, TOP_K, chunk), jnp.int32)] + [pltpu.VMEM((chunk, width), jnp.int32)] * 2
        + [pltpu.SemaphoreType.DMA] * 4,
    )(ys, dest)


def _experts_body(te_ref, nu_ref, xs_ref, w13_ref, w2_ref, ys_ref):
    @pl.when(pl.program_id(0) < nu_ref[0])
    def _():
        lo, hi = _unpack_halves(xs_ref[...])
        h = _dot_halves(lo, hi, w13_ref)
        f = h.shape[1] // 2
        a = _silu(h[:, :f]) * h[:, f:]
        ys_ref[...] = _pack_halves(_dot(a.astype(BF16), w2_ref[...]))


def _moe_experts(xs, tile_expert, n_used, w13, w2):
    n_slots, half = xs.shape
    tile = MOE_SLOT_TILE
    n_e, d, f2 = w13.shape
    return pl.pallas_call(
        _experts_body,
        grid_spec=pltpu.PrefetchScalarGridSpec(
            num_scalar_prefetch=2, grid=(n_slots // tile,),
            in_specs=[pl.BlockSpec((tile, half), lambda i, te, nu: (i, 0)),
                      pl.BlockSpec((None, d, f2), lambda i, te, nu: (te[i], 0, 0)),
                      pl.BlockSpec((None, f2 // 2, d), lambda i, te, nu: (te[i], 0, 0))],
            out_specs=pl.BlockSpec((tile, half), lambda i, te, nu: (i, 0))),
        out_shape=jax.ShapeDtypeStruct((n_slots, half), jnp.int32),
        compiler_params=_params("arbitrary"),
        name="moe_experts",
    )(tile_expert, n_used, xs, w13, w2)


def _combine_body(x_ref, hp_ref, yk_ref, g8_ref, g2_ref, ws1_ref, ws3_ref, ws2_ref, lng_ref, lnb_ref, o_ref):
    lo, hi = _unpack_halves(hp_ref[...])
    a = _silu(_dot_halves(lo, hi, ws1_ref)) * _dot_halves(lo, hi, ws3_ref)
    shared = _dot(a.astype(BF16), ws2_ref[...])
    half = lo.shape[1]
    acc_lo, acc_hi = shared[:, :half], shared[:, half:]
    gt = g8_ref[...].T
    for k in range(TOP_K):
        y_lo, y_hi = _unpack_halves(yk_ref[k])
        acc_lo = acc_lo + gt[:, k:k + 1] * y_lo
        acc_hi = acc_hi + gt[:, k:k + 1] * y_hi
    y = jnp.concatenate([acc_lo, acc_hi], axis=1)
    r_sum = DN_ALPHA * x_ref[...] + (1.0 + g2_ref[...]) * y
    o_ref[...] = _layer_norm(r_sum, lng_ref[...], lnb_ref[...])


def _moe_combine(x, hp, yk, g8, gate2, ws1, ws3, ws2, ln_g, ln_b):
    bsz, seq, d = x.shape
    tm = min(MOE_COMBINE_ROWS, seq)
    nt = seq // tm
    ws1, ws3, ws2 = ws1.astype(BF16), ws3.astype(BF16), ws2.astype(BF16)
    ln_g, ln_b = ln_g[None], ln_b[None]
    full = lambda a: pl.BlockSpec(a.shape, lambda b, i: (0,) * a.ndim)
    return pl.pallas_call(
        _combine_body,
        grid=(bsz, nt),
        in_specs=[
            pl.BlockSpec((None, tm, d), lambda b, i: (b, i, 0)),
            pl.BlockSpec((tm, d // 2), lambda b, i: (b * nt + i, 0)),
            pl.BlockSpec((TOP_K, tm, d // 2), lambda b, i: (0, b * nt + i, 0)),
            pl.BlockSpec((TOP_K, tm), lambda b, i: (0, b * nt + i)),
            pl.BlockSpec((None, 1, d), lambda b, i: (b, 0, 0)),
            full(ws1), full(ws3), full(ws2), full(ln_g), full(ln_b),
        ],
        out_specs=pl.BlockSpec((None, tm, d), lambda b, i: (b, i, 0)),
        out_shape=jax.ShapeDtypeStruct((bsz, seq, d), F32),
        compiler_params=_params("arbitrary", "arbitrary"),
        name="moe_combine",
    )(x, hp, yk, g8, gate2, ws1, ws3, ws2, ln_g, ln_b)


def _moe(x, shift, scale, gate2, w_router, router_bias, w13, w2, ws1, ws3, ws2, ln_g, ln_b):
    bsz, seq, d = x.shape
    tokens = bsz * seq
    n_e = w13.shape[0]
    tile = MOE_SLOT_TILE
    n_slots = tokens * TOP_K + n_e * tile
    hp, e8, p8, g8, counts = _moe_route(x, shift, scale, w_router, router_bias)
    cnt = counts[:, 0].astype(jnp.int32)
    padded = (cnt + (tile - 1)) // tile * tile
    ends = jnp.cumsum(padded)
    base = ends - padded
    tile_start = jnp.arange(n_slots // tile, dtype=jnp.int32) * tile
    tile_expert = jnp.minimum(jnp.sum((tile_start[:, None] >= ends[None, :]).astype(jnp.int32), axis=1), n_e - 1)
    n_used = (ends[-1:] // tile).astype(jnp.int32)
    dest8 = _moe_dest(base, e8, p8)
    chunk = MOE_SC_CHUNK
    dest = dest8.reshape(TOP_K, tokens // chunk, chunk).transpose(1, 0, 2)
    xs = _sc_dispatch(hp, dest, n_slots)
    ys = _moe_experts(xs, tile_expert, n_used, w13, w2)
    yk = _sc_return(ys, dest)
    return _moe_combine(x, hp, yk, g8, gate2, ws1, ws3, ws2, ln_g, ln_b)


def kernel(x, c, ada_w, ada_b, ln1_g, ln1_b, ln2_g, ln2_b, ssm_w_in, ssm_conv_w, ssm_conv_b, ssm_dt_bias,
           ssm_a_log, ssm_d, ssm_norm_w, ssm_w_out, kv_ada_w, kv_ada_b, kv_w, kv_fb, attn_w_q, attn_w_o,
           moe_w_router, moe_bias, moe_w1, moe_w3, moe_w2, moe_ws1, moe_ws3, moe_ws2):
    d = x.shape[-1]
    mods = _adaln(c, ada_w, ada_b)
    kv_mod = _adaln(c, kv_ada_w[None], kv_ada_b[None])[0]
    heads = ssm_dt_bias.shape[-1]
    d_inner = ssm_norm_w.shape[-1]
    conv_dim = ssm_conv_w.shape[-1]
    w13 = jnp.concatenate([moe_w1, moe_w3], axis=-1).astype(BF16)
    w2 = moe_w2.astype(BF16)

    def trunk(x, rows):
        part = lambda m, n: m[rows, None, n * d:(n + 1) * d]
        k_sh = v_sh = cf_parts = None
        for layer in range(DEPTH):
            shift1, scale1, gate1, shift2, scale2, gate2 = (part(mods[layer], n) for n in range(6))
            if layer < N_A_LAYERS:
                a = layer
                z, xbc, dt_raw = _ssm_in(x, shift1, scale1, ssm_w_in[a], d_inner=d_inner, conv_dim=conv_dim,
                                         heads=heads)
                x = _ssm_core(z, xbc, dt_raw, x, gate1, ssm_conv_w[a], ssm_conv_b[a], ssm_dt_bias[a], ssm_a_log[a],
                              ssm_d[a], ssm_norm_w[a], ssm_w_out[a], ln1_g[layer], ln1_b[layer])
            else:
                b = layer - N_A_LAYERS
                x = _attention(x, shift1, scale1, gate1, attn_w_q[b], attn_w_o[b], k_sh, v_sh, cf_parts,
                               ln1_g[layer], ln1_b[layer])
            x = _moe(x, shift2, scale2, gate2, moe_w_router[layer], moe_bias[layer], w13[layer], w2[layer],
                     moe_ws1[layer], moe_ws3[layer], moe_ws2[layer], ln2_g[layer], ln2_b[layer])
            if layer == N_A_LAYERS - 1:
                k_sh, v_sh, cf_parts = _shared_kv(x, part(kv_mod, 0), part(kv_mod, 1), kv_w, kv_fb)
        return x

    per = x.shape[0] // N_STREAMS
    outs = [trunk(x[s * per:(s + 1) * per], slice(s * per, (s + 1) * per)) for s in range(N_STREAMS)]
    return jnp.concatenate(outs, axis=0)
```

```python
import functools

import jax
import jax.numpy as jnp
from jax import lax
from jax.experimental import pallas as pl
from jax.experimental.pallas import tpu as pltpu
from jax.experimental.pallas import tpu_sc as plsc

F32 = jnp.float32
BF16 = jnp.bfloat16
HIGHEST = lax.Precision.HIGHEST

DEPTH = 4
N_A_LAYERS = DEPTH // 2

SSM_HEAD_DIM = 64
SSM_GROUPS = 4
SSM_STATE = 128
SSM_CONV = 4

ATTN_HEAD_DIM = 64

N_EXPERTS = 64
TOP_K = 8
N_EXPERT_GROUPS = 8
TOPK_GROUPS = 4
ROUTED_SCALE = 2.5

DN_ALPHA = (2.0 * DEPTH) ** 0.25
LN_EPS = 1e-5
RMS_EPS = 1e-5
LOG2E = 1.4426950408889634

LANES = 128
SUBLANES = 8
VMEM_LIMIT = 56 * 1024 * 1024

BF16_TILE_ROWS = 2 * SUBLANES
SSD_CHUNK = 128
CONV_COLS = 256
PROJ_ROWS = 256
ATTN_ROWS = 512
ATTN_HEAD_UNROLL = 8
MOE_ROWS = 512
MOE_SLOT_TILE = 512
MOE_TILES_PER_STEP = 4
MOE_COMBINE_ROWS = 512
MOE_SC_CHUNK = 64
N_STREAMS = 2


def _sigmoid(v):
    return 1.0 / (1.0 + jnp.exp(-v))


def _silu(v):
    return v * _sigmoid(v)


def _layer_norm(r, g, b):
    mu = jnp.mean(r, axis=-1, keepdims=True)
    d = r - mu
    var = jnp.mean(d * d, axis=-1, keepdims=True)
    return d * lax.rsqrt(var + LN_EPS) * g + b


def _dot(a, b):
    return jnp.dot(a, b, preferred_element_type=F32)


def _dot_nt(a, b, precision=None):
    return lax.dot_general(a, b, (((1,), (1,)), ((), ())), preferred_element_type=F32, precision=precision)


def _dot_tn(a, b, precision=None):
    return lax.dot_general(a, b, (((0,), (0,)), ((), ())), preferred_element_type=F32, precision=precision)


def _params(*sem):
    return pltpu.CompilerParams(dimension_semantics=sem, vmem_limit_bytes=VMEM_LIMIT)


def _adaln_body(c_ref, w_ref, b_ref, o_ref):
    cond = _silu(c_ref[...])
    o_ref[...] = jnp.dot(cond, w_ref[...], precision=HIGHEST, preferred_element_type=F32) + b_ref[...]


def _adaln(c, w, b):
    nl, d, n = w.shape
    bsz = c.shape[0]
    tn = 1024
    return pl.pallas_call(
        _adaln_body,
        grid=(nl, n // tn),
        in_specs=[
            pl.BlockSpec((bsz, d), lambda l, j: (0, 0)),
            pl.BlockSpec((None, d, tn), lambda l, j: (l, 0, j)),
            pl.BlockSpec((None, 1, tn), lambda l, j: (l, 0, j)),
        ],
        out_specs=pl.BlockSpec((None, bsz, tn), lambda l, j: (l, 0, j)),
        out_shape=jax.ShapeDtypeStruct((nl, bsz, n), F32),
        compiler_params=_params("arbitrary", "arbitrary"),
        name="adaln",
    )(c, w, b.reshape(nl, 1, n))


def _ssm_in_body(x_ref, sh_ref, sc_ref, w_ref, z_ref, xbc_ref, dt_ref, *, d_inner, conv_dim, heads):
    h = (x_ref[...] * (1.0 + sc_ref[...]) + sh_ref[...]).astype(BF16)
    z_ref[...] = _dot(h, w_ref[:, 0:d_inner]).astype(BF16)
    xbc_ref[...] = _dot(h, w_ref[:, d_inner:d_inner + conv_dim]).astype(BF16)
    dt_ref[...] = _dot(h, w_ref[:, d_inner + conv_dim:])[:, :heads]


def _ssm_in(x, shift, scale, w_in, *, d_inner, conv_dim, heads):
    bsz, seq, d = x.shape
    tm = min(PROJ_ROWS, seq)
    n_in = d_inner + conv_dim + heads
    n_pad = -n_in % LANES
    w = jnp.pad(w_in.astype(BF16), ((0, 0), (0, n_pad)))
    vec = pl.BlockSpec((None, 1, d), lambda b, i: (b, 0, 0))
    return pl.pallas_call(
        functools.partial(_ssm_in_body, d_inner=d_inner, conv_dim=conv_dim, heads=heads),
        grid=(bsz, seq // tm),
        in_specs=[
            pl.BlockSpec((None, tm, d), lambda b, i: (b, i, 0)),
            vec, vec,
            pl.BlockSpec((d, n_in + n_pad), lambda b, i: (0, 0)),
        ],
        out_specs=[
            pl.BlockSpec((None, tm, d_inner), lambda b, i: (b, i, 0)),
            pl.BlockSpec((None, tm, conv_dim), lambda b, i: (b, i, 0)),
            pl.BlockSpec((None, tm, heads), lambda b, i: (b, i, 0)),
        ],
        out_shape=[
            jax.ShapeDtypeStruct((bsz, seq, d_inner), BF16),
            jax.ShapeDtypeStruct((bsz, seq, conv_dim), BF16),
            jax.ShapeDtypeStruct((bsz, seq, heads), F32),
        ],
        compiler_params=_params("arbitrary", "arbitrary"),
        name="ssm_in",
    )(x, shift, scale, w)


def _ssm_core_body(z_ref, xbc_ref, dt_ref, x_ref, g1_ref, cw_ref, cb_ref, dtb_ref, alog_ref, dskip_ref,
                   nw_ref, wout_ref, lng_ref, lnb_ref, o_ref, tail_s, state, ybuf, *, heads, d_inner):
    q = SSD_CHUNK
    p_dim, n_dim = SSM_HEAD_DIM, SSM_STATE
    gn = SSM_GROUPS * n_dim
    hpg = heads // SSM_GROUPS
    tail = BF16_TILE_ROWS

    @pl.when(pl.program_id(1) == 0)
    def _():
        tail_s[...] = jnp.zeros(tail_s.shape, BF16)
        state[...] = jnp.zeros(state.shape, F32)

    u_ext = jnp.concatenate([tail_s[...], xbc_ref[...]], axis=0)
    tail_s[...] = xbc_ref[q - tail:q, :]
    tr = lax.broadcasted_iota(jnp.int32, (q, tail + q), 0)
    tc = lax.broadcasted_iota(jnp.int32, (q, tail + q), 1)
    shifts = [jnp.where(tc == tr + (tail - (SSM_CONV - 1) + k), 1.0, 0.0).astype(BF16) for k in range(SSM_CONV - 1)]
    chunks = []
    for c0 in range(0, u_ext.shape[1], CONV_COLS):
        cs = slice(c0, c0 + CONV_COLS)
        acc = cb_ref[:, cs] + cw_ref[SSM_CONV - 1:SSM_CONV, cs] * xbc_ref[:, cs].astype(F32)
        for k in range(SSM_CONV - 1):
            acc = acc + cw_ref[k:k + 1, cs] * _dot(shifts[k], u_ext[:, cs])
        chunks.append(_silu(acc))
    act = jnp.concatenate(chunks, axis=1)

    dt = dt_ref[...] + dtb_ref[...]
    dt = jnp.maximum(dt, 0.0) + jnp.log1p(jnp.exp(-jnp.abs(dt)))
    d_a = dt * (-LOG2E * jnp.exp(alog_ref[...]))
    row = lax.broadcasted_iota(jnp.int32, (q, q), 0)
    col = lax.broadcasted_iota(jnp.int32, (q, q), 1)
    causal = row >= col
    acum = jnp.dot(causal.astype(F32), d_a, precision=HIGHEST, preferred_element_type=F32)
    acum_t = acum.T
    dt_t = dt.T
    er = lax.broadcasted_iota(jnp.int32, (heads, d_inner), 0)
    ec = lax.broadcasted_iota(jnp.int32, (heads, d_inner), 1) // p_dim
    expand = jnp.where(er == ec, 1.0, 0.0).astype(BF16)

    def per_channel(v):
        hi = v.astype(BF16)
        lo = (v - hi.astype(F32)).astype(BF16)
        return _dot(hi, expand) + _dot(lo, expand)

    e_acum_x = per_channel(jnp.exp2(acum))
    w_end_x = per_channel(jnp.exp2(acum[q - 1:q, :] - acum) * dt)
    e_last_x = e_acum_x[q - 1:q, :]
    gw = hpg * p_dim
    lane = lax.broadcasted_iota(jnp.int32, (q, LANES), 1)

    for g in range(SSM_GROUPS):
        gs = slice(g * gw, (g + 1) * gw)
        b_g = act[:, d_inner + g * n_dim:d_inner + (g + 1) * n_dim].astype(BF16)
        c_g = act[:, d_inner + gn + g * n_dim:d_inner + gn + (g + 1) * n_dim].astype(BF16)
        cb = _dot_nt(c_g, b_g)
        xs_g = act[:, gs]
        st = state[g]
        y_off = _dot(c_g, st.astype(BF16)) * e_acum_x[:, gs]
        xw = (xs_g * w_end_x[:, gs]).astype(BF16)
        state[g] = st * e_last_x[:, gs] + _dot_tn(b_g, xw)
        for pr in range(gw // LANES):
            ms = []
            for h in range(g * hpg + 2 * pr, g * hpg + 2 * pr + 2):
                seg = acum[:, h:h + 1] - acum_t[h:h + 1, :]
                ms.append(cb * jnp.exp2(jnp.where(causal, seg, -jnp.inf)) * dt_t[h:h + 1, :])
            pair = xs_g[:, pr * LANES:(pr + 1) * LANES]
            rhs = jnp.concatenate([jnp.where(lane < p_dim, pair, 0.0), jnp.where(lane >= p_dim, pair, 0.0)], axis=0)
            y = _dot(jnp.concatenate(ms, axis=1).astype(BF16), rhs.astype(BF16))
            cs = slice(g * gw + pr * LANES, g * gw + (pr + 1) * LANES)
            ybuf[:, cs] = y + y_off[:, pr * LANES:(pr + 1) * LANES] + dskip_ref[:, cs] * pair

    y = ybuf[...] * _silu(z_ref[...].astype(F32))
    y = y * lax.rsqrt(jnp.mean(y * y, axis=-1, keepdims=True) + RMS_EPS) * nw_ref[...]
    out = _dot(y.astype(BF16), wout_ref[...])
    r_sum = DN_ALPHA * x_ref[...] + (1.0 + g1_ref[...]) * out
    o_ref[...] = _layer_norm(r_sum, lng_ref[...], lnb_ref[...])


def _ssm_core(z, xbc, dt_raw, x, gate1, conv_w, conv_b, dt_bias, a_log, d_skip, norm_w, w_out, ln_g, ln_b):
    bsz, seq, d = x.shape
    d_inner = z.shape[-1]
    conv_dim = xbc.shape[-1]
    heads = dt_raw.shape[-1]
    q = SSD_CHUNK
    rows = lambda n: pl.BlockSpec((None, q, n), lambda b, i: (b, i, 0))
    full = lambda a: pl.BlockSpec(a.shape, lambda b, i: (0,) * a.ndim)
    conv_b, dt_bias, a_log = conv_b[None], dt_bias[None], a_log[None]
    norm_w, ln_g, ln_b = norm_w[None], ln_g[None], ln_b[None]
    w_out = w_out.astype(BF16)
    d_skip = jnp.repeat(d_skip, SSM_HEAD_DIM)[None]
    return pl.pallas_call(
        functools.partial(_ssm_core_body, heads=heads, d_inner=d_inner),
        grid=(bsz, seq // q),
        in_specs=[
            rows(d_inner), rows(conv_dim), rows(heads), rows(d),
            pl.BlockSpec((None, 1, d), lambda b, i: (b, 0, 0)),
            full(conv_w), full(conv_b), full(dt_bias), full(a_log), full(d_skip),
            full(norm_w), full(w_out), full(ln_g), full(ln_b),
        ],
        out_specs=rows(d),
        out_shape=jax.ShapeDtypeStruct((bsz, seq, d), F32),
        scratch_shapes=[
            pltpu.VMEM((BF16_TILE_ROWS, conv_dim), BF16),
            pltpu.VMEM((SSM_GROUPS, SSM_STATE, d_inner // SSM_GROUPS), F32),
            pltpu.VMEM((q, d_inner), F32),
        ],
        compiler_params=_params("arbitrary", "arbitrary"),
        name="ssm_core",
    )(z, xbc, dt_raw, x, gate1, conv_w, conv_b, dt_bias, a_log, d_skip, norm_w, w_out, ln_g, ln_b)


AUG_LANES = LANES


def _split3(v):
    hi = v.astype(BF16)
    r = v - hi.astype(F32)
    mid = r.astype(BF16)
    lo = (r - mid.astype(F32)).astype(BF16)
    return hi, mid, lo


def _place(parts, first_lane, sign):
    heads = parts[0].shape[1]
    r = lax.broadcasted_iota(jnp.int32, (heads, heads * AUG_LANES), 0)
    c = lax.broadcasted_iota(jnp.int32, (heads, heads * AUG_LANES), 1)
    out = None
    for k, part in enumerate(parts):
        mat = jnp.where(c == r * AUG_LANES + (first_lane + k), sign, 0.0).astype(BF16)
        t = _dot(part, mat)
        out = t if out is None else out + t
    return out


def _head_window(m, h, hd):
    base = (h * hd // LANES) * LANES
    w = m[:, base:base + LANES]
    shift = (h * hd) % LANES
    return pltpu.roll(w, LANES - shift, axis=1) if shift else w


def _kv_body(x_ref, sh_ref, sc_ref, w_ref, fb_ref, k_ref, v_ref, cfp_ref, carry, *, d, heads):
    hd = ATTN_HEAD_DIM

    @pl.when(pl.program_id(1) == 0)
    def _():
        carry[...] = jnp.zeros(carry.shape, F32)

    u = (x_ref[...] * (1.0 + sc_ref[...]) + sh_ref[...]).astype(BF16)
    kf = _dot(u, w_ref[:, 0:d])
    vf = _dot(u, w_ref[:, d:2 * d])
    f = _dot(u, w_ref[:, 2 * d:])[:, :heads] + fb_ref[...]
    log_f = jnp.minimum(f, 0.0) - jnp.log1p(jnp.exp(-jnp.abs(f)))
    lt = log_f.T
    tm = lt.shape[1]
    lane_t = lax.broadcasted_iota(jnp.int32, lt.shape, 1)
    step = 1
    while step < tm:
        lt = lt + jnp.where(lane_t >= step, pltpu.roll(lt, step, axis=1), 0.0)
        step *= 2
    lt = lt + carry[...]
    carry[...] = lt[:, tm - 1:tm]
    parts = _split3(lt.T * LOG2E)
    for n in range(3):
        cfp_ref[n] = parts[n]
    extra = _place(parts, hd + 3, -1.0)
    lane = lax.broadcasted_iota(jnp.int32, (tm, AUG_LANES), 1)
    ones_k = jnp.where((lane >= hd) & (lane < hd + 3), 1.0, 0.0)
    ones_v = jnp.where(lane == hd, 1.0, 0.0)
    for h in range(heads):
        ext = extra[:, h * AUG_LANES:(h + 1) * AUG_LANES] + ones_k
        k_ref[h] = jnp.where(lane < hd, _head_window(kf, h, hd), ext).astype(BF16)
        v_ref[h] = jnp.where(lane < hd, _head_window(vf, h, hd), ones_v).astype(BF16)


def _shared_kv(x, shift, scale, kv_w, kv_fb):
    bsz, seq, d = x.shape
    heads = kv_w.shape[1] - 2 * d
    tm = min(PROJ_ROWS, seq)
    n_pad = -kv_w.shape[1] % LANES
    w = jnp.pad(kv_w.astype(BF16), ((0, 0), (0, n_pad)))
    vec = pl.BlockSpec((None, 1, d), lambda b, i: (b, 0, 0))
    aug = pl.BlockSpec((None, heads, tm, AUG_LANES), lambda b, i: (b, 0, i, 0))
    return pl.pallas_call(
        functools.partial(_kv_body, d=d, heads=heads),
        grid=(bsz, seq // tm),
        in_specs=[pl.BlockSpec((None, tm, d), lambda b, i: (b, i, 0)), vec, vec,
                  pl.BlockSpec(w.shape, lambda b, i: (0, 0)),
                  pl.BlockSpec((1, heads), lambda b, i: (0, 0))],
        out_specs=[aug, aug, pl.BlockSpec((None, 3, tm, heads), lambda b, i: (b, 0, i, 0))],
        out_shape=[
            jax.ShapeDtypeStruct((bsz, heads, seq, AUG_LANES), BF16),
            jax.ShapeDtypeStruct((bsz, heads, seq, AUG_LANES), BF16),
            jax.ShapeDtypeStruct((bsz, 3, seq, heads), BF16),
        ],
        scratch_shapes=[pltpu.VMEM((heads, 1), F32)],
        compiler_params=_params("arbitrary", "arbitrary"),
        name="shared_kv",
    )(x, shift, scale, w, kv_fb[None])


def _attn_block(q_s, k_ref, v_ref, m_s, acc_s, *, heads, t, diagonal):
    def head(h, carry):
        s = _dot_nt(q_s[h], k_ref[h])
        if diagonal:
            qi = lax.broadcasted_iota(jnp.int32, (t, t), 0)
            ki = lax.broadcasted_iota(jnp.int32, (t, t), 1)
            s = jnp.where(qi >= ki, s, -jnp.inf)
        m_prev = m_s[h]
        m_new = jnp.maximum(m_prev, jnp.max(s, axis=1, keepdims=True))
        p = jnp.concatenate(
            [jnp.exp2(s[:, c * LANES:(c + 1) * LANES] - m_new) for c in range(t // LANES)], axis=1)
        acc_s[h] = jnp.exp2(m_prev - m_new) * acc_s[h] + _dot(p.astype(BF16), v_ref[h])
        m_s[h] = m_new
        return carry

    lax.fori_loop(0, heads, head, 0, unroll=ATTN_HEAD_UNROLL)


def _attn_body(x_ref, sh_ref, sc_ref, g1_ref, wq_ref, wo_ref, k_ref, v_ref, cfp_ref, lng_ref, lnb_ref,
               o_ref, q_s, m_s, acc_s, o_s, *, heads, t):
    i = pl.program_id(1)
    j = pl.program_id(2)
    hd = ATTN_HEAD_DIM

    @pl.when(j == 0)
    def _():
        h = (x_ref[...] * (1.0 + sc_ref[...]) + sh_ref[...]).astype(BF16)
        qf = _dot(h, wq_ref[...]) * (hd ** -0.5 * LOG2E)
        extra = _place([cfp_ref[n] for n in range(3)], hd, 1.0)
        lane = lax.broadcasted_iota(jnp.int32, (t, AUG_LANES), 1)
        ones_q = jnp.where((lane >= hd + 3) & (lane < hd + 6), 1.0, 0.0)
        for hh in range(heads):
            ext = extra[:, hh * AUG_LANES:(hh + 1) * AUG_LANES] + ones_q
            q_s[hh] = jnp.where(lane < hd, _head_window(qf, hh, hd), ext).astype(BF16)
        m_s[...] = jnp.full(m_s.shape, -jnp.inf, F32)
        acc_s[...] = jnp.zeros(acc_s.shape, F32)

    block = functools.partial(_attn_block, q_s, k_ref, v_ref, m_s, acc_s, heads=heads, t=t)
    pl.when(j < i)(functools.partial(block, diagonal=False))
    pl.when(j == i)(functools.partial(block, diagonal=True))

    @pl.when(j == pl.num_programs(2) - 1)
    def _():
        for hh in range(heads):
            a = acc_s[hh]
            o_s[:, hh * hd:(hh + 1) * hd] = (a / a[:, hd:hd + 1])[:, :hd].astype(BF16)
        y = _dot(o_s[...], wo_ref[...])
        r_sum = DN_ALPHA * x_ref[...] + (1.0 + g1_ref[...]) * y
        o_ref[...] = _layer_norm(r_sum, lng_ref[...], lnb_ref[...])


def _attention(x, shift, scale, gate1, w_q, w_o, k_aug, v_aug, cf_parts, ln_g, ln_b):
    bsz, seq, d = x.shape
    heads = k_aug.shape[1]
    t = min(ATTN_ROWS, seq)
    vec = pl.BlockSpec((None, 1, d), lambda b, i, j: (b, 0, 0))
    full = lambda a: pl.BlockSpec(a.shape, lambda b, i, j: (0,) * a.ndim)
    kv_spec = pl.BlockSpec((None, heads, t, AUG_LANES), lambda b, i, j: (b, 0, jnp.minimum(j, i), 0))
    w_q, w_o, ln_g, ln_b = w_q.astype(BF16), w_o.astype(BF16), ln_g[None], ln_b[None]
    return pl.pallas_call(
        functools.partial(_attn_body, heads=heads, t=t),
        grid=(bsz, seq // t, seq // t),
        in_specs=[
            pl.BlockSpec((None, t, d), lambda b, i, j: (b, i, 0)),
            vec, vec, vec, full(w_q), full(w_o), kv_spec, kv_spec,
            pl.BlockSpec((None, 3, t, heads), lambda b, i, j: (b, 0, i, 0)),
            full(ln_g), full(ln_b),
        ],
        out_specs=pl.BlockSpec((None, t, d), lambda b, i, j: (b, i, 0)),
        out_shape=jax.ShapeDtypeStruct((bsz, seq, d), F32),
        scratch_shapes=[
            pltpu.VMEM((heads, t, AUG_LANES), BF16),
            pltpu.VMEM((heads, t, LANES), F32),
            pltpu.VMEM((heads, t, AUG_LANES), F32),
            pltpu.VMEM((t, d), BF16),
        ],
        compiler_params=_params("arbitrary", "arbitrary", "arbitrary"),
        name="fox_attention",
    )(x, shift, scale, gate1, w_q, w_o, k_aug, v_aug, cf_parts, ln_g, ln_b)


def _route(sel, scores):
    n_e, t = sel.shape
    per = n_e // N_EXPERT_GROUPS
    sub = lax.broadcasted_iota(jnp.int32, (per, t), 0)
    neg = -jnp.inf
    gs = jnp.zeros((N_EXPERT_GROUPS, t), F32)
    gidx = lax.broadcasted_iota(jnp.int32, (N_EXPERT_GROUPS, t), 0)
    for g in range(N_EXPERT_GROUPS):
        v = sel[g * per:(g + 1) * per, :]
        m1 = jnp.max(v, axis=0, keepdims=True)
        first = jnp.min(jnp.where(v == m1, sub, per), axis=0, keepdims=True)
        m2 = jnp.max(jnp.where(sub == first, neg, v), axis=0, keepdims=True)
        gs = jnp.where(gidx == g, m1 + m2, gs)
    grank = jnp.zeros((N_EXPERT_GROUPS, t), jnp.int32)
    for g in range(N_EXPERT_GROUPS):
        other = gs[g:g + 1, :]
        beats = (other > gs) | ((other >= gs) & (gidx > g))
        grank = grank + jnp.where(beats, 1, 0)
    masked = jnp.concatenate(
        [jnp.where(grank[g:g + 1, :] < TOPK_GROUPS, sel[g * per:(g + 1) * per, :], neg)
         for g in range(N_EXPERT_GROUPS)], axis=0)
    eidx = lax.broadcasted_iota(jnp.int32, (n_e, t), 0)
    work = masked
    w = jnp.zeros((n_e, t), F32)
    chosen = jnp.zeros((n_e, t), F32)
    for _ in range(TOP_K):
        top = jnp.max(work, axis=0, keepdims=True)
        first = jnp.min(jnp.where(work == top, eidx, n_e), axis=0, keepdims=True)
        pick = eidx == first
        w = jnp.where(pick, scores, w)
        chosen = jnp.where(pick, 1.0, chosen)
        work = jnp.where(pick, neg, work)
    return w / jnp.sum(w, axis=0, keepdims=True) * ROUTED_SCALE, chosen


def _pack_halves(v):
    half = v.shape[1] // 2
    bits = lambda a: lax.bitcast_convert_type(a.astype(BF16).astype(F32), jnp.uint32)
    word = (bits(v[:, half:]) & jnp.uint32(0xFFFF0000)) | (bits(v[:, :half]) >> 16)
    return lax.bitcast_convert_type(word, jnp.int32)


def _unpack_halves(w):
    u = lax.bitcast_convert_type(w, jnp.uint32)
    lo = lax.bitcast_convert_type(u << 16, F32)
    hi = lax.bitcast_convert_type(u & jnp.uint32(0xFFFF0000), F32)
    return lo, hi


def _dot_halves(lo, hi, w_ref):
    half = lo.shape[1]
    return _dot(lo.astype(BF16), w_ref[:half, :]) + _dot(hi.astype(BF16), w_ref[half:, :])


def _route_body(x_ref, sh_ref, sc_ref, wr_ref, rb_ref, hp_ref, e8_ref, p8_ref, g8_ref, cnt_ref, carry):
    @pl.when((pl.program_id(0) == 0) & (pl.program_id(1) == 0))
    def _():
        carry[...] = jnp.zeros(carry.shape, F32)

    h = x_ref[...] * (1.0 + sc_ref[...]) + sh_ref[...]
    hp_ref[...] = _pack_halves(h)
    scores = _sigmoid(_dot_nt(wr_ref[...], h, precision=HIGHEST))
    gate, chosen = _route(scores + rb_ref[...], scores)
    n_e, tm = chosen.shape
    lane = lax.broadcasted_iota(jnp.int32, (n_e, tm), 1)
    incl = chosen
    step = 1
    while step < tm:
        incl = incl + jnp.where(lane >= step, pltpu.roll(incl, step, axis=1), 0.0)
        step *= 2
    pos = carry[...] + incl - chosen
    carry[...] = carry[...] + incl[:, tm - 1:tm]
    cnt_ref[...] = jnp.broadcast_to(carry[...], cnt_ref.shape)
    er = lax.broadcasted_iota(jnp.int32, (n_e, n_e), 0)
    ec = lax.broadcasted_iota(jnp.int32, (n_e, n_e), 1)
    before = jnp.where(ec < er, 1.0, 0.0).astype(BF16)
    rank = _dot(before, chosen.astype(BF16))
    eidx = lax.broadcasted_iota(jnp.int32, (n_e, tm), 0).astype(F32)
    row = lax.broadcasted_iota(jnp.int32, (TOP_K, tm), 0)
    e8 = jnp.zeros((TOP_K, tm), F32)
    p8 = jnp.zeros((TOP_K, tm), F32)
    g8 = jnp.zeros((TOP_K, tm), F32)
    for k in range(TOP_K):
        sel = jnp.where((rank == k) & (chosen > 0.0), 1.0, 0.0)
        e8 = jnp.where(row == k, jnp.sum(sel * eidx, axis=0, keepdims=True), e8)
        p8 = jnp.where(row == k, jnp.sum(sel * pos, axis=0, keepdims=True), p8)
        g8 = jnp.where(row == k, jnp.sum(sel * gate, axis=0, keepdims=True), g8)
    e8_ref[...] = e8.astype(jnp.int32)
    p8_ref[...] = p8.astype(jnp.int32)
    g8_ref[...] = g8


def _moe_route(x, shift, scale, w_router, router_bias):
    bsz, seq, d = x.shape
    n_e = w_router.shape[1]
    tm = min(MOE_ROWS, seq)
    nt = seq // tm
    tokens = bsz * seq
    vec = pl.BlockSpec((None, 1, d), lambda b, i: (b, 0, 0))
    per_tok = pl.BlockSpec((TOP_K, tm), lambda b, i: (0, b * nt + i))
    wr_t = w_router.T
    rb = router_bias[:, None]
    return pl.pallas_call(
        _route_body,
        grid=(bsz, nt),
        in_specs=[pl.BlockSpec((None, tm, d), lambda b, i: (b, i, 0)), vec, vec,
                  pl.BlockSpec(wr_t.shape, lambda b, i: (0, 0)), pl.BlockSpec(rb.shape, lambda b, i: (0, 0))],
        out_specs=[pl.BlockSpec((tm, d // 2), lambda b, i: (b * nt + i, 0)), per_tok, per_tok, per_tok,
                   pl.BlockSpec((n_e, LANES), lambda b, i: (0, 0))],
        out_shape=[
            jax.ShapeDtypeStruct((tokens, d // 2), jnp.int32),
            jax.ShapeDtypeStruct((TOP_K, tokens), jnp.int32),
            jax.ShapeDtypeStruct((TOP_K, tokens), jnp.int32),
            jax.ShapeDtypeStruct((TOP_K, tokens), F32),
            jax.ShapeDtypeStruct((n_e, LANES), F32),
        ],
        scratch_shapes=[pltpu.VMEM((n_e, 1), F32)],
        compiler_params=_params("arbitrary", "arbitrary"),
        name="moe_route",
    )(x, shift, scale, wr_t, rb)


def _dest_body(base_ref, e8_ref, p8_ref, o_ref, *, n_e):
    e8 = e8_ref[...]
    dest = p8_ref[...]
    for e in range(n_e):
        dest = dest + jnp.where(e8 == e, base_ref[e], 0)
    o_ref[...] = dest


def _moe_dest(base, e8, p8):
    tokens = e8.shape[1]
    tb = min(2048, tokens)
    blk = pl.BlockSpec((TOP_K, tb), lambda i, base: (0, i))
    return pl.pallas_call(
        functools.partial(_dest_body, n_e=base.shape[0]),
        grid_spec=pltpu.PrefetchScalarGridSpec(num_scalar_prefetch=1, grid=(tokens // tb,), in_specs=[blk, blk],
                                               out_specs=blk),
        out_shape=jax.ShapeDtypeStruct((TOP_K, tokens), jnp.int32),
        compiler_params=_params("arbitrary"),
        name="moe_dest",
    )(base, e8, p8)


def _sc_mesh():
    return plsc.VectorSubcoreMesh(core_axis_name="c", subcore_axis_name="s")


def _sc_dispatch(rows, dest, n_slots):
    tokens, width = rows.shape
    chunk = dest.shape[2]
    sc = plsc.get_sparse_core_info()
    n_cores, n_workers = sc.num_cores, sc.num_cores * sc.num_subcores
    t_per_w = tokens // n_workers
    n_chunks = t_per_w // chunk
    assert n_chunks * chunk * n_workers == tokens and n_chunks % 2 == 0

    def body(rows_hbm, dest_hbm, out_hbm, idx0, idx1, buf0, buf1, rs0, rs1, ws0, ws1):
        idx, bufs, rsem, wsem = (idx0, idx1), (buf0, buf1), (rs0, rs1), (ws0, ws1)
        wid = lax.axis_index("s") * n_cores + lax.axis_index("c")

        def read(i, b):
            return pltpu.make_async_copy(rows_hbm.at[pl.ds(wid * t_per_w + i * chunk, chunk)], bufs[b], rsem[b])

        def scatter(b, k):
            return pltpu.make_async_copy(bufs[b], out_hbm.at[idx[b].at[k]], wsem[b])

        read(0, 0).start()

        def pair(g, carry):
            for b in range(2):
                i = g * 2 + b

                @pl.when(i + 1 < n_chunks)
                def _():
                    @pl.when(i >= 1)
                    def _():
                        for k in range(TOP_K):
                            scatter(1 - b, k).wait()
                    read(i + 1, 1 - b).start()

                pltpu.sync_copy(dest_hbm.at[wid * n_chunks + i], idx[b])
                read(i, b).wait()
                for k in range(TOP_K):
                    scatter(b, k).start()
            return carry

        lax.fori_loop(0, n_chunks // 2, pair, 0)
        for b in range(2):
            for k in range(TOP_K):
                scatter(b, k).wait()

    return pl.kernel(
        body, mesh=_sc_mesh(), out_type=jax.ShapeDtypeStruct((n_slots, width), jnp.int32),
        scratch_types=[pltpu.VMEM((TOP_K, chunk), jnp.int32)] * 2 + [pltpu.VMEM((chunk, width), jnp.int32)] * 2
        + [pltpu.SemaphoreType.DMA] * 4,
    )(rows, dest)


def _sc_return(ys, dest):
    width = ys.shape[1]
    n_all, _, chunk = dest.shape
    tokens = n_all * chunk
    sc = plsc.get_sparse_core_info()
    n_cores, n_workers = sc.num_cores, sc.num_cores * sc.num_subcores
    t_per_w = tokens // n_workers
    n_chunks = t_per_w // chunk
    n_items = n_chunks * TOP_K
    assert n_chunks * chunk * n_workers == tokens

    def body(ys_hbm, dest_hbm, out_hbm, idx, buf0, buf1, gs0, gs1, ws0, ws1):
        bufs, gsem, wsem = (buf0, buf1), (gs0, gs1), (ws0, ws1)
        wid = lax.axis_index("s") * n_cores + lax.axis_index("c")
        pltpu.sync_copy(dest_hbm.at[pl.ds(wid * n_chunks, n_chunks)], idx)

        def gather(it, b):
            return pltpu.make_async_copy(ys_hbm.at[idx.at[it // TOP_K, it % TOP_K]], bufs[b], gsem[b])

        def write(it, b):
            dst = out_hbm.at[it % TOP_K, pl.ds(wid * t_per_w + (it // TOP_K) * chunk, chunk)]
            return pltpu.make_async_copy(bufs[b], dst, wsem[b])

        gather(0, 0).start()

        def pair(g, carry):
            for b in range(2):
                it = g * 2 + b

                @pl.when(it + 1 < n_items)
                def _():
                    @pl.when(it >= 1)
                    def _():
                        write(it - 1, 1 - b).wait()
                    gather(it + 1, 1 - b).start()

                gather(it, b).wait()
                write(it, b).start()
            return carry

        lax.fori_loop(0, n_items // 2, pair, 0)
        write(n_items - 2, 0).wait()
        write(n_items - 1, 1).wait()

    return pl.kernel(
        body, mesh=_sc_mesh(), out_type=jax.ShapeDtypeStruct((TOP_K, tokens, width), jnp.int32),
        scratch_types=[pltpu.VMEM((n_chunks, TOP_K, chunk), jnp.int32)] + [pltpu.VMEM((chunk, width), jnp.int32)] * 2
        + [pltpu.SemaphoreType.DMA] * 4,
    )(ys, dest)


def _experts_body(te_ref, nu_ref, xs_ref, *refs, tile, group):
    w13_refs, w2_refs, ys_ref = refs[:group], refs[group:2 * group], refs[2 * group]
    @pl.when(pl.program_id(0) * group < nu_ref[0])
    def _():
        for j in range(group):
            rows = pl.ds(j * tile, tile)
            lo, hi = _unpack_halves(xs_ref[rows, :])
            h = _dot_halves(lo, hi, w13_refs[j])
            f = h.shape[1] // 2
            a = _silu(h[:, :f]) * h[:, f:]
            ys_ref[rows, :] = _pack_halves(_dot(a.astype(BF16), w2_refs[j][...]))


def _moe_experts(xs, tile_expert, n_used, w13, w2):
    n_slots, half = xs.shape
    tile, group = MOE_SLOT_TILE, MOE_TILES_PER_STEP
    n_e, d, f2 = w13.shape
    owner = lambda j: (lambda i, te, nu: (te[i * group + j], 0, 0))
    rows = pl.BlockSpec((group * tile, half), lambda i, te, nu: (i, 0))
    return pl.pallas_call(
        functools.partial(_experts_body, tile=tile, group=group),
        grid_spec=pltpu.PrefetchScalarGridSpec(
            num_scalar_prefetch=2, grid=(n_slots // (group * tile),),
            in_specs=[rows] + [pl.BlockSpec((None, d, f2), owner(j)) for j in range(group)]
            + [pl.BlockSpec((None, f2 // 2, d), owner(j)) for j in range(group)],
            out_specs=rows),
        out_shape=jax.ShapeDtypeStruct((n_slots, half), jnp.int32),
        compiler_params=_params("arbitrary"),
        name="moe_experts",
    )(tile_expert, n_used, xs, *([w13] * group), *([w2] * group))


def _combine_body(x_ref, hp_ref, yk_ref, g8_ref, g2_ref, ws1_ref, ws3_ref, ws2_ref, lng_ref, lnb_ref, o_ref):
    lo, hi = _unpack_halves(hp_ref[...])
    a = _silu(_dot_halves(lo, hi, ws1_ref)) * _dot_halves(lo, hi, ws3_ref)
    shared = _dot(a.astype(BF16), ws2_ref[...])
    half = lo.shape[1]
    acc_lo, acc_hi = shared[:, :half], shared[:, half:]
    gt = g8_ref[...].T
    for k in range(TOP_K):
        y_lo, y_hi = _unpack_halves(yk_ref[k])
        acc_lo = acc_lo + gt[:, k:k + 1] * y_lo
        acc_hi = acc_hi + gt[:, k:k + 1] * y_hi
    y = jnp.concatenate([acc_lo, acc_hi], axis=1)
    r_sum = DN_ALPHA * x_ref[...] + (1.0 + g2_ref[...]) * y
    o_ref[...] = _layer_norm(r_sum, lng_ref[...], lnb_ref[...])


def _moe_combine(x, hp, yk, g8, gate2, ws1, ws3, ws2, ln_g, ln_b):
    bsz, seq, d = x.shape
    tm = min(MOE_COMBINE_ROWS, seq)
    nt = seq // tm
    ws1, ws3, ws2 = ws1.astype(BF16), ws3.astype(BF16), ws2.astype(BF16)
    ln_g, ln_b = ln_g[None], ln_b[None]
    full = lambda a: pl.BlockSpec(a.shape, lambda b, i: (0,) * a.ndim)
    return pl.pallas_call(
        _combine_body,
        grid=(bsz, nt),
        in_specs=[
            pl.BlockSpec((None, tm, d), lambda b, i: (b, i, 0)),
            pl.BlockSpec((tm, d // 2), lambda b, i: (b * nt + i, 0)),
            pl.BlockSpec((TOP_K, tm, d // 2), lambda b, i: (0, b * nt + i, 0)),
            pl.BlockSpec((TOP_K, tm), lambda b, i: (0, b * nt + i)),
            pl.BlockSpec((None, 1, d), lambda b, i: (b, 0, 0)),
            full(ws1), full(ws3), full(ws2), full(ln_g), full(ln_b),
        ],
        out_specs=pl.BlockSpec((None, tm, d), lambda b, i: (b, i, 0)),
        out_shape=jax.ShapeDtypeStruct((bsz, seq, d), F32),
        compiler_params=_params("arbitrary", "arbitrary"),
        name="moe_combine",
    )(x, hp, yk, g8, gate2, ws1, ws3, ws2, ln_g, ln_b)


def _moe(x, shift, scale, gate2, w_router, router_bias, w13, w2, ws1, ws3, ws2, ln_g, ln_b):
    bsz, seq, d = x.shape
    tokens = bsz * seq
    n_e = w13.shape[0]
    tile = MOE_SLOT_TILE
    n_slots = tokens * TOP_K + n_e * tile
    hp, e8, p8, g8, counts = _moe_route(x, shift, scale, w_router, router_bias)
    cnt = counts[:, 0].astype(jnp.int32)
    padded = (cnt + (tile - 1)) // tile * tile
    ends = jnp.cumsum(padded)
    base = ends - padded
    tile_start = jnp.arange(n_slots // tile, dtype=jnp.int32) * tile
    tile_expert = jnp.minimum(jnp.sum((tile_start[:, None] >= ends[None, :]).astype(jnp.int32), axis=1), n_e - 1)
    n_used = (ends[-1:] // tile).astype(jnp.int32)
    dest8 = _moe_dest(base, e8, p8)
    chunk = MOE_SC_CHUNK
    dest = dest8.reshape(TOP_K, tokens // chunk, chunk).transpose(1, 0, 2)
    xs = _sc_dispatch(hp, dest, n_slots)
    ys = _moe_experts(xs, tile_expert, n_used, w13, w2)
    yk = _sc_return(ys, dest)
    return _moe_combine(x, hp, yk, g8, gate2, ws1, ws3, ws2, ln_g, ln_b)


def kernel(x, c, ada_w, ada_b, ln1_g, ln1_b, ln2_g, ln2_b, ssm_w_in, ssm_conv_w, ssm_conv_b, ssm_dt_bias,
           ssm_a_log, ssm_d, ssm_norm_w, ssm_w_out, kv_ada_w, kv_ada_b, kv_w, kv_fb, attn_w_q, attn_w_o,
           moe_w_router, moe_bias, moe_w1, moe_w3, moe_w2, moe_ws1, moe_ws3, moe_ws2):
    d = x.shape[-1]
    mods = _adaln(c, ada_w, ada_b)
    kv_mod = _adaln(c, kv_ada_w[None], kv_ada_b[None])[0]
    heads = ssm_dt_bias.shape[-1]
    d_inner = ssm_norm_w.shape[-1]
    conv_dim = ssm_conv_w.shape[-1]
    w13 = jnp.concatenate([moe_w1, moe_w3], axis=-1).astype(BF16)
    w2 = moe_w2.astype(BF16)

    def trunk(x, rows):
        part = lambda m, n: m[rows, None, n * d:(n + 1) * d]
        k_sh = v_sh = cf_parts = None
        for layer in range(DEPTH):
            shift1, scale1, gate1, shift2, scale2, gate2 = (part(mods[layer], n) for n in range(6))
            if layer < N_A_LAYERS:
                a = layer
                z, xbc, dt_raw = _ssm_in(x, shift1, scale1, ssm_w_in[a], d_inner=d_inner, conv_dim=conv_dim,
                                         heads=heads)
                x = _ssm_core(z, xbc, dt_raw, x, gate1, ssm_conv_w[a], ssm_conv_b[a], ssm_dt_bias[a], ssm_a_log[a],
                              ssm_d[a], ssm_norm_w[a], ssm_w_out[a], ln1_g[layer], ln1_b[layer])
            else:
                b = layer - N_A_LAYERS
                x = _attention(x, shift1, scale1, gate1, attn_w_q[b], attn_w_o[b], k_sh, v_sh, cf_parts,
                               ln1_g[layer], ln1_b[layer])
            x = _moe(x, shift2, scale2, gate2, moe_w_router[layer], moe_bias[layer], w13[layer], w2[layer],
                     moe_ws1[layer], moe_ws3[layer], moe_ws2[layer], ln2_g[layer], ln2_b[layer])
            if layer == N_A_LAYERS - 1:
                k_sh, v_sh, cf_parts = _shared_kv(x, part(kv_mod, 0), part(kv_mod, 1), kv_w, kv_fb)
        return x

    per = x.shape[0] // N_STREAMS
    outs = [trunk(x[s * per:(s + 1) * per], slice(s * per, (s + 1) * per)) for s in range(N_STREAMS)]
    return jnp.concatenate(outs, axis=0)
```

```python
import functools

import jax
import jax.numpy as jnp
from jax import lax
from jax.experimental import pallas as pl
from jax.experimental.pallas import tpu as pltpu
from jax.experimental.pallas import tpu_sc as plsc

F32 = jnp.float32
BF16 = jnp.bfloat16
HIGHEST = lax.Precision.HIGHEST

DEPTH = 4
N_A_LAYERS = DEPTH // 2

SSM_HEAD_DIM = 64
SSM_GROUPS = 4
SSM_STATE = 128
SSM_CONV = 4

ATTN_HEAD_DIM = 64

N_EXPERTS = 64
TOP_K = 8
N_EXPERT_GROUPS = 8
TOPK_GROUPS = 4
ROUTED_SCALE = 2.5

DN_ALPHA = (2.0 * DEPTH) ** 0.25
LN_EPS = 1e-5
RMS_EPS = 1e-5
LOG2E = 1.4426950408889634

LANES = 128
SUBLANES = 8
VMEM_LIMIT = 56 * 1024 * 1024

BF16_TILE_ROWS = 2 * SUBLANES
SSD_CHUNK = 128
CONV_COLS = 256
PROJ_ROWS = 256
ATTN_ROWS = 512
ATTN_HEAD_UNROLL = 16
MOE_ROWS = 512
MOE_SLOT_TILE = 512
MOE_TILES_PER_STEP = 8
MOE_COMBINE_ROWS = 512
MOE_SC_CHUNK = 64
N_STREAMS = 2


def _sigmoid(v):
    return 1.0 / (1.0 + jnp.exp(-v))


def _silu(v):
    return v * _sigmoid(v)


def _layer_norm(r, g, b):
    mu = jnp.mean(r, axis=-1, keepdims=True)
    d = r - mu
    var = jnp.mean(d * d, axis=-1, keepdims=True)
    return d * lax.rsqrt(var + LN_EPS) * g + b


def _dot(a, b):
    return jnp.dot(a, b, preferred_element_type=F32)


def _dot_nt(a, b, precision=None):
    return lax.dot_general(a, b, (((1,), (1,)), ((), ())), preferred_element_type=F32, precision=precision)


def _dot_tn(a, b, precision=None):
    return lax.dot_general(a, b, (((0,), (0,)), ((), ())), preferred_element_type=F32, precision=precision)


def _params(*sem):
    return pltpu.CompilerParams(dimension_semantics=sem, vmem_limit_bytes=VMEM_LIMIT)


def _adaln_body(c_ref, w_ref, b_ref, o_ref):
    cond = _silu(c_ref[...])
    o_ref[...] = jnp.dot(cond, w_ref[...], precision=HIGHEST, preferred_element_type=F32) + b_ref[...]


def _adaln(c, w, b):
    nl, d, n = w.shape
    bsz = c.shape[0]
    tn = 1024
    return pl.pallas_call(
        _adaln_body,
        grid=(nl, n // tn),
        in_specs=[
            pl.BlockSpec((bsz, d), lambda l, j: (0, 0)),
            pl.BlockSpec((None, d, tn), lambda l, j: (l, 0, j)),
            pl.BlockSpec((None, 1, tn), lambda l, j: (l, 0, j)),
        ],
        out_specs=pl.BlockSpec((None, bsz, tn), lambda l, j: (l, 0, j)),
        out_shape=jax.ShapeDtypeStruct((nl, bsz, n), F32),
        compiler_params=_params("arbitrary", "arbitrary"),
        name="adaln",
    )(c, w, b.reshape(nl, 1, n))


def _ssm_in_body(x_ref, sh_ref, sc_ref, w_ref, z_ref, xbc_ref, dt_ref, *, d_inner, conv_dim, heads):
    h = (x_ref[...] * (1.0 + sc_ref[...]) + sh_ref[...]).astype(BF16)
    z_ref[...] = _dot(h, w_ref[:, 0:d_inner]).astype(BF16)
    xbc_ref[...] = _dot(h, w_ref[:, d_inner:d_inner + conv_dim]).astype(BF16)
    dt_ref[...] = _dot(h, w_ref[:, d_inner + conv_dim:])[:, :heads]


def _ssm_in(x, shift, scale, w_in, *, d_inner, conv_dim, heads):
    bsz, seq, d = x.shape
    tm = min(PROJ_ROWS, seq)
    n_in = d_inner + conv_dim + heads
    n_pad = -n_in % LANES
    w = jnp.pad(w_in.astype(BF16), ((0, 0), (0, n_pad)))
    vec = pl.BlockSpec((None, 1, d), lambda b, i: (b, 0, 0))
    return pl.pallas_call(
        functools.partial(_ssm_in_body, d_inner=d_inner, conv_dim=conv_dim, heads=heads),
        grid=(bsz, seq // tm),
        in_specs=[
            pl.BlockSpec((None, tm, d), lambda b, i: (b, i, 0)),
            vec, vec,
            pl.BlockSpec((d, n_in + n_pad), lambda b, i: (0, 0)),
        ],
        out_specs=[
            pl.BlockSpec((None, tm, d_inner), lambda b, i: (b, i, 0)),
            pl.BlockSpec((None, tm, conv_dim), lambda b, i: (b, i, 0)),
            pl.BlockSpec((None, tm, heads), lambda b, i: (b, i, 0)),
        ],
        out_shape=[
            jax.ShapeDtypeStruct((bsz, seq, d_inner), BF16),
            jax.ShapeDtypeStruct((bsz, seq, conv_dim), BF16),
            jax.ShapeDtypeStruct((bsz, seq, heads), F32),
        ],
        compiler_params=_params("arbitrary", "arbitrary"),
        name="ssm_in",
    )(x, shift, scale, w)


def _ssm_core_body(z_ref, xbc_ref, dt_ref, x_ref, g1_ref, cw_ref, cb_ref, dtb_ref, alog_ref, dskip_ref,
                   nw_ref, wout_ref, lng_ref, lnb_ref, o_ref, tail_s, state, ybuf, *, heads, d_inner):
    q = SSD_CHUNK
    p_dim, n_dim = SSM_HEAD_DIM, SSM_STATE
    gn = SSM_GROUPS * n_dim
    hpg = heads // SSM_GROUPS
    tail = BF16_TILE_ROWS

    @pl.when(pl.program_id(1) == 0)
    def _():
        tail_s[...] = jnp.zeros(tail_s.shape, BF16)
        state[...] = jnp.zeros(state.shape, F32)

    u_ext = jnp.concatenate([tail_s[...], xbc_ref[...]], axis=0)
    tail_s[...] = xbc_ref[q - tail:q, :]
    tr = lax.broadcasted_iota(jnp.int32, (q, tail + q), 0)
    tc = lax.broadcasted_iota(jnp.int32, (q, tail + q), 1)
    shifts = [jnp.where(tc == tr + (tail - (SSM_CONV - 1) + k), 1.0, 0.0).astype(BF16) for k in range(SSM_CONV - 1)]
    chunks = []
    for c0 in range(0, u_ext.shape[1], CONV_COLS):
        cs = slice(c0, c0 + CONV_COLS)
        acc = cb_ref[:, cs] + cw_ref[SSM_CONV - 1:SSM_CONV, cs] * xbc_ref[:, cs].astype(F32)
        for k in range(SSM_CONV - 1):
            acc = acc + cw_ref[k:k + 1, cs] * _dot(shifts[k], u_ext[:, cs])
        chunks.append(_silu(acc))
    act = jnp.concatenate(chunks, axis=1)

    dt = dt_ref[...] + dtb_ref[...]
    dt = jnp.maximum(dt, 0.0) + jnp.log1p(jnp.exp(-jnp.abs(dt)))
    d_a = dt * (-LOG2E * jnp.exp(alog_ref[...]))
    row = lax.broadcasted_iota(jnp.int32, (q, q), 0)
    col = lax.broadcasted_iota(jnp.int32, (q, q), 1)
    causal = row >= col
    acum = jnp.dot(causal.astype(F32), d_a, precision=HIGHEST, preferred_element_type=F32)
    acum_t = acum.T
    dt_t = dt.T
    er = lax.broadcasted_iota(jnp.int32, (heads, d_inner), 0)
    ec = lax.broadcasted_iota(jnp.int32, (heads, d_inner), 1) // p_dim
    expand = jnp.where(er == ec, 1.0, 0.0).astype(BF16)

    def per_channel(v):
        hi = v.astype(BF16)
        lo = (v - hi.astype(F32)).astype(BF16)
        return _dot(hi, expand) + _dot(lo, expand)

    e_acum_x = per_channel(jnp.exp2(acum))
    w_end_x = per_channel(jnp.exp2(acum[q - 1:q, :] - acum) * dt)
    e_last_x = e_acum_x[q - 1:q, :]
    gw = hpg * p_dim
    lane = lax.broadcasted_iota(jnp.int32, (q, LANES), 1)

    for g in range(SSM_GROUPS):
        gs = slice(g * gw, (g + 1) * gw)
        b_g = act[:, d_inner + g * n_dim:d_inner + (g + 1) * n_dim].astype(BF16)
        c_g = act[:, d_inner + gn + g * n_dim:d_inner + gn + (g + 1) * n_dim].astype(BF16)
        cb = _dot_nt(c_g, b_g)
        xs_g = act[:, gs]
        st = state[g]
        y_off = _dot(c_g, st.astype(BF16)) * e_acum_x[:, gs]
        xw = (xs_g * w_end_x[:, gs]).astype(BF16)
        state[g] = st * e_last_x[:, gs] + _dot_tn(b_g, xw)
        for pr in range(gw // LANES):
            ms = []
            for h in range(g * hpg + 2 * pr, g * hpg + 2 * pr + 2):
                seg = acum[:, h:h + 1] - acum_t[h:h + 1, :]
                ms.append(cb * jnp.exp2(jnp.where(causal, seg, -jnp.inf)) * dt_t[h:h + 1, :])
            pair = xs_g[:, pr * LANES:(pr + 1) * LANES]
            rhs = jnp.concatenate([jnp.where(lane < p_dim, pair, 0.0), jnp.where(lane >= p_dim, pair, 0.0)], axis=0)
            y = _dot(jnp.concatenate(ms, axis=1).astype(BF16), rhs.astype(BF16))
            cs = slice(g * gw + pr * LANES, g * gw + (pr + 1) * LANES)
            ybuf[:, cs] = y + y_off[:, pr * LANES:(pr + 1) * LANES] + dskip_ref[:, cs] * pair

    y = ybuf[...] * _silu(z_ref[...].astype(F32))
    y = y * lax.rsqrt(jnp.mean(y * y, axis=-1, keepdims=True) + RMS_EPS) * nw_ref[...]
    out = _dot(y.astype(BF16), wout_ref[...])
    r_sum = DN_ALPHA * x_ref[...] + (1.0 + g1_ref[...]) * out
    o_ref[...] = _layer_norm(r_sum, lng_ref[...], lnb_ref[...])


def _ssm_core(z, xbc, dt_raw, x, gate1, conv_w, conv_b, dt_bias, a_log, d_skip, norm_w, w_out, ln_g, ln_b):
    bsz, seq, d = x.shape
    d_inner = z.shape[-1]
    conv_dim = xbc.shape[-1]
    heads = dt_raw.shape[-1]
    q = SSD_CHUNK
    rows = lambda n: pl.BlockSpec((None, q, n), lambda b, i: (b, i, 0))
    full = lambda a: pl.BlockSpec(a.shape, lambda b, i: (0,) * a.ndim)
    conv_b, dt_bias, a_log = conv_b[None], dt_bias[None], a_log[None]
    norm_w, ln_g, ln_b = norm_w[None], ln_g[None], ln_b[None]
    w_out = w_out.astype(BF16)
    d_skip = jnp.repeat(d_skip, SSM_HEAD_DIM)[None]
    return pl.pallas_call(
        functools.partial(_ssm_core_body, heads=heads, d_inner=d_inner),
        grid=(bsz, seq // q),
        in_specs=[
            rows(d_inner), rows(conv_dim), rows(heads), rows(d),
            pl.BlockSpec((None, 1, d), lambda b, i: (b, 0, 0)),
            full(conv_w), full(conv_b), full(dt_bias), full(a_log), full(d_skip),
            full(norm_w), full(w_out), full(ln_g), full(ln_b),
        ],
        out_specs=rows(d),
        out_shape=jax.ShapeDtypeStruct((bsz, seq, d), F32),
        scratch_shapes=[
            pltpu.VMEM((BF16_TILE_ROWS, conv_dim), BF16),
            pltpu.VMEM((SSM_GROUPS, SSM_STATE, d_inner // SSM_GROUPS), F32),
            pltpu.VMEM((q, d_inner), F32),
        ],
        compiler_params=_params("arbitrary", "arbitrary"),
        name="ssm_core",
    )(z, xbc, dt_raw, x, gate1, conv_w, conv_b, dt_bias, a_log, d_skip, norm_w, w_out, ln_g, ln_b)


AUG_LANES = LANES


def _split3(v):
    hi = v.astype(BF16)
    r = v - hi.astype(F32)
    mid = r.astype(BF16)
    lo = (r - mid.astype(F32)).astype(BF16)
    return hi, mid, lo


PART_STRIDE = LANES // 16


def _place(parts, sign):
    heads = parts[0].shape[1]
    r = lax.broadcasted_iota(jnp.int32, (heads, LANES), 0)
    c = lax.broadcasted_iota(jnp.int32, (heads, LANES), 1)
    out = None
    for k, part in enumerate(parts):
        t = _dot(part, jnp.where(c == r * PART_STRIDE + k, sign, 0.0).astype(BF16))
        out = t if out is None else out + t
    return out


def _head_parts(compact, h, first_lane, lane):
    rolled = pltpu.roll(compact, (first_lane - h * PART_STRIDE) % LANES, axis=1)
    return jnp.where((lane >= first_lane) & (lane < first_lane + 3), rolled, 0.0)


def _head_window(m, h, hd):
    base = (h * hd // LANES) * LANES
    w = m[:, base:base + LANES]
    shift = (h * hd) % LANES
    return pltpu.roll(w, LANES - shift, axis=1) if shift else w


def _kv_body(x_ref, sh_ref, sc_ref, w_ref, fb_ref, k_ref, v_ref, cfp_ref, carry, *, d, heads):
    hd = ATTN_HEAD_DIM

    @pl.when(pl.program_id(1) == 0)
    def _():
        carry[...] = jnp.zeros(carry.shape, F32)

    u = (x_ref[...] * (1.0 + sc_ref[...]) + sh_ref[...]).astype(BF16)
    kf = _dot(u, w_ref[:, 0:d])
    vf = _dot(u, w_ref[:, d:2 * d])
    f = _dot(u, w_ref[:, 2 * d:])[:, :heads] + fb_ref[...]
    log_f = jnp.minimum(f, 0.0) - jnp.log1p(jnp.exp(-jnp.abs(f)))
    lt = log_f.T
    tm = lt.shape[1]
    lane_t = lax.broadcasted_iota(jnp.int32, lt.shape, 1)
    step = 1
    while step < tm:
        lt = lt + jnp.where(lane_t >= step, pltpu.roll(lt, step, axis=1), 0.0)
        step *= 2
    lt = lt + carry[...]
    carry[...] = lt[:, tm - 1:tm]
    parts = _split3(lt.T * LOG2E)
    for n in range(3):
        cfp_ref[n] = parts[n]
    compact = _place(parts, -1.0)
    lane = lax.broadcasted_iota(jnp.int32, (tm, AUG_LANES), 1)
    ones_k = jnp.where((lane >= hd) & (lane < hd + 3), 1.0, 0.0)
    ones_v = jnp.where(lane == hd, 1.0, 0.0)
    for h in range(heads):
        ext = _head_parts(compact, h, hd + 3, lane) + ones_k
        k_ref[h] = jnp.where(lane < hd, _head_window(kf, h, hd), ext).astype(BF16)
        v_ref[h] = jnp.where(lane < hd, _head_window(vf, h, hd), ones_v).astype(BF16)


def _shared_kv(x, shift, scale, kv_w, kv_fb):
    bsz, seq, d = x.shape
    heads = kv_w.shape[1] - 2 * d
    tm = min(PROJ_ROWS, seq)
    n_pad = -kv_w.shape[1] % LANES
    w = jnp.pad(kv_w.astype(BF16), ((0, 0), (0, n_pad)))
    vec = pl.BlockSpec((None, 1, d), lambda b, i: (b, 0, 0))
    aug = pl.BlockSpec((None, heads, tm, AUG_LANES), lambda b, i: (b, 0, i, 0))
    return pl.pallas_call(
        functools.partial(_kv_body, d=d, heads=heads),
        grid=(bsz, seq // tm),
        in_specs=[pl.BlockSpec((None, tm, d), lambda b, i: (b, i, 0)), vec, vec,
                  pl.BlockSpec(w.shape, lambda b, i: (0, 0)),
                  pl.BlockSpec((1, heads), lambda b, i: (0, 0))],
        out_specs=[aug, aug, pl.BlockSpec((None, 3, tm, heads), lambda b, i: (b, 0, i, 0))],
        out_shape=[
            jax.ShapeDtypeStruct((bsz, heads, seq, AUG_LANES), BF16),
            jax.ShapeDtypeStruct((bsz, heads, seq, AUG_LANES), BF16),
            jax.ShapeDtypeStruct((bsz, 3, seq, heads), BF16),
        ],
        scratch_shapes=[pltpu.VMEM((heads, 1), F32)],
        compiler_params=_params("arbitrary", "arbitrary"),
        name="shared_kv",
    )(x, shift, scale, w, kv_fb[None])


def _attn_block(q_s, k_ref, v_ref, m_s, acc_s, *, heads, t, diagonal):
    def head(h, carry):
        s = _dot_nt(q_s[h], k_ref[h])
        if diagonal:
            qi = lax.broadcasted_iota(jnp.int32, (t, t), 0)
            ki = lax.broadcasted_iota(jnp.int32, (t, t), 1)
            s = jnp.where(qi >= ki, s, -jnp.inf)
        m_prev = m_s[h]
        m_new = jnp.maximum(m_prev, jnp.max(s, axis=1, keepdims=True))
        p = jnp.concatenate(
            [jnp.exp2(s[:, c * LANES:(c + 1) * LANES] - m_new) for c in range(t // LANES)], axis=1)
        acc_s[h] = jnp.exp2(m_prev - m_new) * acc_s[h] + _dot(p.astype(BF16), v_ref[h])
        m_s[h] = m_new
        return carry

    lax.fori_loop(0, heads, head, 0, unroll=ATTN_HEAD_UNROLL)


def _attn_body(x_ref, sh_ref, sc_ref, g1_ref, wq_ref, wo_ref, k_ref, v_ref, cfp_ref, lng_ref, lnb_ref,
               o_ref, q_s, m_s, acc_s, o_s, *, heads, t):
    i = pl.program_id(1)
    j = pl.program_id(2)
    hd = ATTN_HEAD_DIM

    @pl.when(j == 0)
    def _():
        h = (x_ref[...] * (1.0 + sc_ref[...]) + sh_ref[...]).astype(BF16)
        qf = _dot(h, wq_ref[...]) * (hd ** -0.5 * LOG2E)
        compact = _place([cfp_ref[n] for n in range(3)], 1.0)
        lane = lax.broadcasted_iota(jnp.int32, (t, AUG_LANES), 1)
        ones_q = jnp.where((lane >= hd + 3) & (lane < hd + 6), 1.0, 0.0)
        for hh in range(heads):
            ext = _head_parts(compact, hh, hd, lane) + ones_q
            q_s[hh] = jnp.where(lane < hd, _head_window(qf, hh, hd), ext).astype(BF16)
        m_s[...] = jnp.full(m_s.shape, -jnp.inf, F32)
        acc_s[...] = jnp.zeros(acc_s.shape, F32)

    block = functools.partial(_attn_block, q_s, k_ref, v_ref, m_s, acc_s, heads=heads, t=t)
    pl.when(j < i)(functools.partial(block, diagonal=False))
    pl.when(j == i)(functools.partial(block, diagonal=True))

    @pl.when(j == pl.num_programs(2) - 1)
    def _():
        for hh in range(heads):
            a = acc_s[hh]
            o_s[:, hh * hd:(hh + 1) * hd] = (a / a[:, hd:hd + 1])[:, :hd].astype(BF16)
        y = _dot(o_s[...], wo_ref[...])
        r_sum = DN_ALPHA * x_ref[...] + (1.0 + g1_ref[...]) * y
        o_ref[...] = _layer_norm(r_sum, lng_ref[...], lnb_ref[...])


def _attention(x, shift, scale, gate1, w_q, w_o, k_aug, v_aug, cf_parts, ln_g, ln_b):
    bsz, seq, d = x.shape
    heads = k_aug.shape[1]
    t = min(ATTN_ROWS, seq)
    vec = pl.BlockSpec((None, 1, d), lambda b, i, j: (b, 0, 0))
    full = lambda a: pl.BlockSpec(a.shape, lambda b, i, j: (0,) * a.ndim)
    kv_spec = pl.BlockSpec((None, heads, t, AUG_LANES), lambda b, i, j: (b, 0, jnp.minimum(j, i), 0))
    w_q, w_o, ln_g, ln_b = w_q.astype(BF16), w_o.astype(BF16), ln_g[None], ln_b[None]
    return pl.pallas_call(
        functools.partial(_attn_body, heads=heads, t=t),
        grid=(bsz, seq // t, seq // t),
        in_specs=[
            pl.BlockSpec((None, t, d), lambda b, i, j: (b, i, 0)),
            vec, vec, vec, full(w_q), full(w_o), kv_spec, kv_spec,
            pl.BlockSpec((None, 3, t, heads), lambda b, i, j: (b, 0, i, 0)),
            full(ln_g), full(ln_b),
        ],
        out_specs=pl.BlockSpec((None, t, d), lambda b, i, j: (b, i, 0)),
        out_shape=jax.ShapeDtypeStruct((bsz, seq, d), F32),
        scratch_shapes=[
            pltpu.VMEM((heads, t, AUG_LANES), BF16),
            pltpu.VMEM((heads, t, LANES), F32),
            pltpu.VMEM((heads, t, AUG_LANES), F32),
            pltpu.VMEM((t, d), BF16),
        ],
        compiler_params=_params("arbitrary", "arbitrary", "arbitrary"),
        name="fox_attention",
    )(x, shift, scale, gate1, w_q, w_o, k_aug, v_aug, cf_parts, ln_g, ln_b)


def _route(sel, scores):
    n_e, t = sel.shape
    per = n_e // N_EXPERT_GROUPS
    sub = lax.broadcasted_iota(jnp.int32, (per, t), 0)
    neg = -jnp.inf
    gs = jnp.zeros((N_EXPERT_GROUPS, t), F32)
    gidx = lax.broadcasted_iota(jnp.int32, (N_EXPERT_GROUPS, t), 0)
    for g in range(N_EXPERT_GROUPS):
        v = sel[g * per:(g + 1) * per, :]
        m1 = jnp.max(v, axis=0, keepdims=True)
        first = jnp.min(jnp.where(v == m1, sub, per), axis=0, keepdims=True)
        m2 = jnp.max(jnp.where(sub == first, neg, v), axis=0, keepdims=True)
        gs = jnp.where(gidx == g, m1 + m2, gs)
    grank = jnp.zeros((N_EXPERT_GROUPS, t), jnp.int32)
    for g in range(N_EXPERT_GROUPS):
        other = gs[g:g + 1, :]
        beats = (other > gs) | ((other >= gs) & (gidx > g))
        grank = grank + jnp.where(beats, 1, 0)
    masked = jnp.concatenate(
        [jnp.where(grank[g:g + 1, :] < TOPK_GROUPS, sel[g * per:(g + 1) * per, :], neg)
         for g in range(N_EXPERT_GROUPS)], axis=0)
    eidx = lax.broadcasted_iota(jnp.int32, (n_e, t), 0)
    work = masked
    w = jnp.zeros((n_e, t), F32)
    chosen = jnp.zeros((n_e, t), F32)
    for _ in range(TOP_K):
        top = jnp.max(work, axis=0, keepdims=True)
        first = jnp.min(jnp.where(work == top, eidx, n_e), axis=0, keepdims=True)
        pick = eidx == first
        w = jnp.where(pick, scores, w)
        chosen = jnp.where(pick, 1.0, chosen)
        work = jnp.where(pick, neg, work)
    return w / jnp.sum(w, axis=0, keepdims=True) * ROUTED_SCALE, chosen


def _pack_halves(v):
    half = v.shape[1] // 2
    bits = lambda a: lax.bitcast_convert_type(a.astype(BF16).astype(F32), jnp.uint32)
    word = (bits(v[:, half:]) & jnp.uint32(0xFFFF0000)) | (bits(v[:, :half]) >> 16)
    return lax.bitcast_convert_type(word, jnp.int32)


def _unpack_halves(w):
    u = lax.bitcast_convert_type(w, jnp.uint32)
    lo = lax.bitcast_convert_type(u << 16, F32)
    hi = lax.bitcast_convert_type(u & jnp.uint32(0xFFFF0000), F32)
    return lo, hi


def _dot_halves(lo, hi, w_ref):
    half = lo.shape[1]
    return _dot(lo.astype(BF16), w_ref[:half, :]) + _dot(hi.astype(BF16), w_ref[half:, :])


def _route_body(x_ref, sh_ref, sc_ref, wr_ref, rb_ref, hp_ref, e8_ref, p8_ref, g8_ref, cnt_ref, carry):
    @pl.when((pl.program_id(0) == 0) & (pl.program_id(1) == 0))
    def _():
        carry[...] = jnp.zeros(carry.shape, F32)

    h = x_ref[...] * (1.0 + sc_ref[...]) + sh_ref[...]
    hp_ref[...] = _pack_halves(h)
    scores = _sigmoid(_dot_nt(wr_ref[...], h, precision=HIGHEST))
    gate, chosen = _route(scores + rb_ref[...], scores)
    n_e, tm = chosen.shape
    lane = lax.broadcasted_iota(jnp.int32, (n_e, tm), 1)
    incl = chosen
    step = 1
    while step < tm:
        incl = incl + jnp.where(lane >= step, pltpu.roll(incl, step, axis=1), 0.0)
        step *= 2
    pos = carry[...] + incl - chosen
    carry[...] = carry[...] + incl[:, tm - 1:tm]
    cnt_ref[...] = jnp.broadcast_to(carry[...], cnt_ref.shape)
    er = lax.broadcasted_iota(jnp.int32, (n_e, n_e), 0)
    ec = lax.broadcasted_iota(jnp.int32, (n_e, n_e), 1)
    before = jnp.where(ec < er, 1.0, 0.0).astype(BF16)
    rank = _dot(before, chosen.astype(BF16))
    eidx = lax.broadcasted_iota(jnp.int32, (n_e, tm), 0).astype(F32)
    row = lax.broadcasted_iota(jnp.int32, (TOP_K, tm), 0)
    e8 = jnp.zeros((TOP_K, tm), F32)
    p8 = jnp.zeros((TOP_K, tm), F32)
    g8 = jnp.zeros((TOP_K, tm), F32)
    for k in range(TOP_K):
        sel = jnp.where((rank == k) & (chosen > 0.0), 1.0, 0.0)
        e8 = jnp.where(row == k, jnp.sum(sel * eidx, axis=0, keepdims=True), e8)
        p8 = jnp.where(row == k, jnp.sum(sel * pos, axis=0, keepdims=True), p8)
        g8 = jnp.where(row == k, jnp.sum(sel * gate, axis=0, keepdims=True), g8)
    e8_ref[...] = e8.astype(jnp.int32)
    p8_ref[...] = p8.astype(jnp.int32)
    g8_ref[...] = g8


def _moe_route(x, shift, scale, w_router, router_bias):
    bsz, seq, d = x.shape
    n_e = w_router.shape[1]
    tm = min(MOE_ROWS, seq)
    nt = seq // tm
    tokens = bsz * seq
    vec = pl.BlockSpec((None, 1, d), lambda b, i: (b, 0, 0))
    per_tok = pl.BlockSpec((TOP_K, tm), lambda b, i: (0, b * nt + i))
    wr_t = w_router.T
    rb = router_bias[:, None]
    return pl.pallas_call(
        _route_body,
        grid=(bsz, nt),
        in_specs=[pl.BlockSpec((None, tm, d), lambda b, i: (b, i, 0)), vec, vec,
                  pl.BlockSpec(wr_t.shape, lambda b, i: (0, 0)), pl.BlockSpec(rb.shape, lambda b, i: (0, 0))],
        out_specs=[pl.BlockSpec((tm, d // 2), lambda b, i: (b * nt + i, 0)), per_tok, per_tok, per_tok,
                   pl.BlockSpec((n_e, LANES), lambda b, i: (0, 0))],
        out_shape=[
            jax.ShapeDtypeStruct((tokens, d // 2), jnp.int32),
            jax.ShapeDtypeStruct((TOP_K, tokens), jnp.int32),
            jax.ShapeDtypeStruct((TOP_K, tokens), jnp.int32),
            jax.ShapeDtypeStruct((TOP_K, tokens), F32),
            jax.ShapeDtypeStruct((n_e, LANES), F32),
        ],
        scratch_shapes=[pltpu.VMEM((n_e, 1), F32)],
        compiler_params=_params("arbitrary", "arbitrary"),
        name="moe_route",
    )(x, shift, scale, wr_t, rb)


def _dest_body(base_ref, e8_ref, p8_ref, o_ref, *, n_e):
    e8 = e8_ref[...]
    dest = p8_ref[...]
    for e in range(n_e):
        dest = dest + jnp.where(e8 == e, base_ref[e], 0)
    o_ref[...] = dest


def _moe_dest(base, e8, p8):
    tokens = e8.shape[1]
    tb = min(2048, tokens)
    blk = pl.BlockSpec((TOP_K, tb), lambda i, base: (0, i))
    return pl.pallas_call(
        functools.partial(_dest_body, n_e=base.shape[0]),
        grid_spec=pltpu.PrefetchScalarGridSpec(num_scalar_prefetch=1, grid=(tokens // tb,), in_specs=[blk, blk],
                                               out_specs=blk),
        out_shape=jax.ShapeDtypeStruct((TOP_K, tokens), jnp.int32),
        compiler_params=_params("arbitrary"),
        name="moe_dest",
    )(base, e8, p8)


def _sc_mesh():
    return plsc.VectorSubcoreMesh(core_axis_name="c", subcore_axis_name="s")


def _sc_dispatch(rows, dest, n_slots):
    tokens, width = rows.shape
    chunk = dest.shape[2]
    sc = plsc.get_sparse_core_info()
    n_cores, n_workers = sc.num_cores, sc.num_cores * sc.num_subcores
    t_per_w = tokens // n_workers
    n_chunks = t_per_w // chunk
    assert n_chunks * chunk * n_workers == tokens and n_chunks % 2 == 0

    def body(rows_hbm, dest_hbm, out_hbm, idx0, idx1, buf0, buf1, rs0, rs1, ws0, ws1):
        idx, bufs, rsem, wsem = (idx0, idx1), (buf0, buf1), (rs0, rs1), (ws0, ws1)
        wid = lax.axis_index("s") * n_cores + lax.axis_index("c")

        def read(i, b):
            return pltpu.make_async_copy(rows_hbm.at[pl.ds(wid * t_per_w + i * chunk, chunk)], bufs[b], rsem[b])

        def scatter(b, k):
            return pltpu.make_async_copy(bufs[b], out_hbm.at[idx[b].at[k]], wsem[b])

        read(0, 0).start()

        def pair(g, carry):
            for b in range(2):
                i = g * 2 + b

                @pl.when(i + 1 < n_chunks)
                def _():
                    @pl.when(i >= 1)
                    def _():
                        for k in range(TOP_K):
                            scatter(1 - b, k).wait()
                    read(i + 1, 1 - b).start()

                pltpu.sync_copy(dest_hbm.at[wid * n_chunks + i], idx[b])
                read(i, b).wait()
                for k in range(TOP_K):
                    scatter(b, k).start()
            return carry

        lax.fori_loop(0, n_chunks // 2, pair, 0)
        for b in range(2):
            for k in range(TOP_K):
                scatter(b, k).wait()

    return pl.kernel(
        body, mesh=_sc_mesh(), out_type=jax.ShapeDtypeStruct((n_slots, width), jnp.int32),
        scratch_types=[pltpu.VMEM((TOP_K, chunk), jnp.int32)] * 2 + [pltpu.VMEM((chunk, width), jnp.int32)] * 2
        + [pltpu.SemaphoreType.DMA] * 4,
    )(rows, dest)


def _sc_return(ys, dest):
    width = ys.shape[1]
    n_all, _, chunk = dest.shape
    tokens = n_all * chunk
    sc = plsc.get_sparse_core_info()
    n_cores, n_workers = sc.num_cores, sc.num_cores * sc.num_subcores
    t_per_w = tokens // n_workers
    n_chunks = t_per_w // chunk
    n_items = n_chunks * TOP_K
    assert n_chunks * chunk * n_workers == tokens

    def body(ys_hbm, dest_hbm, out_hbm, idx, buf0, buf1, gs0, gs1, ws0, ws1):
        bufs, gsem, wsem = (buf0, buf1), (gs0, gs1), (ws0, ws1)
        wid = lax.axis_index("s") * n_cores + lax.axis_index("c")
        pltpu.sync_copy(dest_hbm.at[pl.ds(wid * n_chunks, n_chunks)], idx)

        def gather(it, b):
            return pltpu.make_async_copy(ys_hbm.at[idx.at[it // TOP_K, it % TOP_K]], bufs[b], gsem[b])

        def write(it, b):
            dst = out_hbm.at[it % TOP_K, pl.ds(wid * t_per_w + (it // TOP_K) * chunk, chunk)]
            return pltpu.make_async_copy(bufs[b], dst, wsem[b])

        gather(0, 0).start()

        def pair(g, carry):
            for b in range(2):
                it = g * 2 + b

                @pl.when(it + 1 < n_items)
                def _():
                    @pl.when(it >= 1)
                    def _():
                        write(it - 1, 1 - b).wait()
                    gather(it + 1, 1 - b).start()

                gather(it, b).wait()
                write(it, b).start()
            return carry

        lax.fori_loop(0, n_items // 2, pair, 0)
        write(n_items - 2, 0).wait()
        write(n_items - 1, 1).wait()

    return pl.kernel(
        body, mesh=_sc_mesh(), out_type=jax.ShapeDtypeStruct((TOP_K, tokens, width), jnp.int32),
        scratch_types=[pltpu.VMEM((n_chunks, TOP_K, chunk), jnp.int32)] + [pltpu.VMEM((chunk, width), jnp.int32)] * 2
        + [pltpu.SemaphoreType.DMA] * 4,
    )(ys, dest)


def _experts_body(te_ref, nu_ref, xs_ref, *refs, tile, group):
    w13_refs, w2_refs, ys_ref = refs[:group], refs[group:2 * group], refs[2 * group]
    @pl.when(pl.program_id(0) * group < nu_ref[0])
    def _():
        for j in range(group):
            rows = pl.ds(j * tile, tile)
            lo, hi = _unpack_halves(xs_ref[rows, :])
            h = _dot_halves(lo, hi, w13_refs[j])
            f = h.shape[1] // 2
            a = _silu(h[:, :f]) * h[:, f:]
            ys_ref[rows, :] = _pack_halves(_dot(a.astype(BF16), w2_refs[j][...]))


def _moe_experts(xs, tile_expert, n_used, w13, w2):
    n_slots, half = xs.shape
    tile, group = MOE_SLOT_TILE, MOE_TILES_PER_STEP
    n_e, d, f2 = w13.shape
    owner = lambda j: (lambda i, te, nu: (te[i * group + j], 0, 0))
    rows = pl.BlockSpec((group * tile, half), lambda i, te, nu: (i, 0))
    return pl.pallas_call(
        functools.partial(_experts_body, tile=tile, group=group),
        grid_spec=pltpu.PrefetchScalarGridSpec(
            num_scalar_prefetch=2, grid=(n_slots // (group * tile),),
            in_specs=[rows] + [pl.BlockSpec((None, d, f2), owner(j)) for j in range(group)]
            + [pl.BlockSpec((None, f2 // 2, d), owner(j)) for j in range(group)],
            out_specs=rows),
        out_shape=jax.ShapeDtypeStruct((n_slots, half), jnp.int32),
        compiler_params=_params("arbitrary"),
        name="moe_experts",
    )(tile_expert, n_used, xs, *([w13] * group), *([w2] * group))


def _combine_body(x_ref, hp_ref, yk_ref, g8_ref, g2_ref, ws1_ref, ws3_ref, ws2_ref, lng_ref, lnb_ref, o_ref):
    lo, hi = _unpack_halves(hp_ref[...])
    a = _silu(_dot_halves(lo, hi, ws1_ref)) * _dot_halves(lo, hi, ws3_ref)
    shared = _dot(a.astype(BF16), ws2_ref[...])
    half = lo.shape[1]
    acc_lo, acc_hi = shared[:, :half], shared[:, half:]
    gt = g8_ref[...].T
    for k in range(TOP_K):
        y_lo, y_hi = _unpack_halves(yk_ref[k])
        acc_lo = acc_lo + gt[:, k:k + 1] * y_lo
        acc_hi = acc_hi + gt[:, k:k + 1] * y_hi
    y = jnp.concatenate([acc_lo, acc_hi], axis=1)
    r_sum = DN_ALPHA * x_ref[...] + (1.0 + g2_ref[...]) * y
    o_ref[...] = _layer_norm(r_sum, lng_ref[...], lnb_ref[...])


def _moe_combine(x, hp, yk, g8, gate2, ws1, ws3, ws2, ln_g, ln_b):
    bsz, seq, d = x.shape
    tm = min(MOE_COMBINE_ROWS, seq)
    nt = seq // tm
    ws1, ws3, ws2 = ws1.astype(BF16), ws3.astype(BF16), ws2.astype(BF16)
    ln_g, ln_b = ln_g[None], ln_b[None]
    full = lambda a: pl.BlockSpec(a.shape, lambda b, i: (0,) * a.ndim)
    return pl.pallas_call(
        _combine_body,
        grid=(bsz, nt),
        in_specs=[
            pl.BlockSpec((None, tm, d), lambda b, i: (b, i, 0)),
            pl.BlockSpec((tm, d // 2), lambda b, i: (b * nt + i, 0)),
            pl.BlockSpec((TOP_K, tm, d // 2), lambda b, i: (0, b * nt + i, 0)),
            pl.BlockSpec((TOP_K, tm), lambda b, i: (0, b * nt + i)),
            pl.BlockSpec((None, 1, d), lambda b, i: (b, 0, 0)),
            full(ws1), full(ws3), full(ws2), full(ln_g), full(ln_b),
        ],
        out_specs=pl.BlockSpec((None, tm, d), lambda b, i: (b, i, 0)),
        out_shape=jax.ShapeDtypeStruct((bsz, seq, d), F32),
        compiler_params=_params("arbitrary", "arbitrary"),
        name="moe_combine",
    )(x, hp, yk, g8, gate2, ws1, ws3, ws2, ln_g, ln_b)


def _moe(x, shift, scale, gate2, w_router, router_bias, w13, w2, ws1, ws3, ws2, ln_g, ln_b):
    bsz, seq, d = x.shape
    tokens = bsz * seq
    n_e = w13.shape[0]
    tile = MOE_SLOT_TILE
    n_slots = tokens * TOP_K + n_e * tile
    hp, e8, p8, g8, counts = _moe_route(x, shift, scale, w_router, router_bias)
    cnt = counts[:, 0].astype(jnp.int32)
    padded = (cnt + (tile - 1)) // tile * tile
    ends = jnp.cumsum(padded)
    base = ends - padded
    tile_start = jnp.arange(n_slots // tile, dtype=jnp.int32) * tile
    tile_expert = jnp.minimum(jnp.sum((tile_start[:, None] >= ends[None, :]).astype(jnp.int32), axis=1), n_e - 1)
    n_used = (ends[-1:] // tile).astype(jnp.int32)
    dest8 = _moe_dest(base, e8, p8)
    chunk = MOE_SC_CHUNK
    dest = dest8.reshape(TOP_K, tokens // chunk, chunk).transpose(1, 0, 2)
    xs = _sc_dispatch(hp, dest, n_slots)
    ys = _moe_experts(xs, tile_expert, n_used, w13, w2)
    yk = _sc_return(ys, dest)
    return _moe_combine(x, hp, yk, g8, gate2, ws1, ws3, ws2, ln_g, ln_b)


def kernel(x, c, ada_w, ada_b, ln1_g, ln1_b, ln2_g, ln2_b, ssm_w_in, ssm_conv_w, ssm_conv_b, ssm_dt_bias,
           ssm_a_log, ssm_d, ssm_norm_w, ssm_w_out, kv_ada_w, kv_ada_b, kv_w, kv_fb, attn_w_q, attn_w_o,
           moe_w_router, moe_bias, moe_w1, moe_w3, moe_w2, moe_ws1, moe_ws3, moe_ws2):
    d = x.shape[-1]
    mods = _adaln(c, ada_w, ada_b)
    kv_mod = _adaln(c, kv_ada_w[None], kv_ada_b[None])[0]
    heads = ssm_dt_bias.shape[-1]
    d_inner = ssm_norm_w.shape[-1]
    conv_dim = ssm_conv_w.shape[-1]
    w13 = jnp.concatenate([moe_w1, moe_w3], axis=-1).astype(BF16)
    w2 = moe_w2.astype(BF16)

    def layer_step(layer, x, rows, kv):
        part = lambda m, n: m[rows, None, n * d:(n + 1) * d]
        shift1, scale1, gate1, shift2, scale2, gate2 = (part(mods[layer], n) for n in range(6))
        if layer < N_A_LAYERS:
            a = layer
            z, xbc, dt_raw = _ssm_in(x, shift1, scale1, ssm_w_in[a], d_inner=d_inner, conv_dim=conv_dim, heads=heads)
            x = _ssm_core(z, xbc, dt_raw, x, gate1, ssm_conv_w[a], ssm_conv_b[a], ssm_dt_bias[a], ssm_a_log[a],
                          ssm_d[a], ssm_norm_w[a], ssm_w_out[a], ln1_g[layer], ln1_b[layer])
        else:
            b = layer - N_A_LAYERS
            x = _attention(x, shift1, scale1, gate1, attn_w_q[b], attn_w_o[b], *kv, ln1_g[layer], ln1_b[layer])
        x = _moe(x, shift2, scale2, gate2, moe_w_router[layer], moe_bias[layer], w13[layer], w2[layer],
                 moe_ws1[layer], moe_ws3[layer], moe_ws2[layer], ln2_g[layer], ln2_b[layer])
        if layer == N_A_LAYERS - 1:
            kv = _shared_kv(x, part(kv_mod, 0), part(kv_mod, 1), kv_w, kv_fb)
        return x, kv

    per = x.shape[0] // N_STREAMS
    outs = []
    for s in range(N_STREAMS):
        rows, kv = slice(s * per, (s + 1) * per), None
        xs = x[rows]
        for layer in range(DEPTH):
            xs, kv = layer_step(layer, xs, rows, kv)
        outs.append(xs)
    return jnp.concatenate(outs, axis=0)
```

```python
import functools

import jax
import jax.numpy as jnp
from jax import lax
from jax.experimental import pallas as pl
from jax.experimental.pallas import tpu as pltpu
from jax.experimental.pallas import tpu_sc as plsc

F32 = jnp.float32
BF16 = jnp.bfloat16
HIGHEST = lax.Precision.HIGHEST

DEPTH = 4
N_A_LAYERS = DEPTH // 2

SSM_HEAD_DIM = 64
SSM_GROUPS = 4
SSM_STATE = 128
SSM_CONV = 4

ATTN_HEAD_DIM = 64

N_EXPERTS = 64
TOP_K = 8
N_EXPERT_GROUPS = 8
TOPK_GROUPS = 4
ROUTED_SCALE = 2.5

DN_ALPHA = (2.0 * DEPTH) ** 0.25
LN_EPS = 1e-5
RMS_EPS = 1e-5
LOG2E = 1.4426950408889634

LANES = 128
SUBLANES = 8
VMEM_LIMIT = 56 * 1024 * 1024

BF16_TILE_ROWS = 2 * SUBLANES
SSD_CHUNK = 128
CONV_COLS = 256
PROJ_ROWS = 512
ATTN_ROWS = 512
ATTN_HEAD_UNROLL = 16
MOE_ROWS = 512
MOE_SLOT_TILE = 512
MOE_TILES_PER_STEP = 8
MOE_COMBINE_ROWS = 512
MOE_SC_CHUNK = 64
N_STREAMS = 2


def _sigmoid(v):
    return 1.0 / (1.0 + jnp.exp(-v))


def _silu(v):
    return v * _sigmoid(v)


def _layer_norm(r, g, b):
    mu = jnp.mean(r, axis=-1, keepdims=True)
    d = r - mu
    var = jnp.mean(d * d, axis=-1, keepdims=True)
    return d * lax.rsqrt(var + LN_EPS) * g + b


def _dot(a, b):
    return jnp.dot(a, b, preferred_element_type=F32)


def _dot_nt(a, b, precision=None):
    return lax.dot_general(a, b, (((1,), (1,)), ((), ())), preferred_element_type=F32, precision=precision)


def _dot_tn(a, b, precision=None):
    return lax.dot_general(a, b, (((0,), (0,)), ((), ())), preferred_element_type=F32, precision=precision)


def _params(*sem):
    return pltpu.CompilerParams(dimension_semantics=sem, vmem_limit_bytes=VMEM_LIMIT)


def _adaln_body(c_ref, w_ref, b_ref, o_ref):
    cond = _silu(c_ref[...])
    o_ref[...] = jnp.dot(cond, w_ref[...], precision=HIGHEST, preferred_element_type=F32) + b_ref[...]


def _adaln(c, w, b):
    nl, d, n = w.shape
    bsz = c.shape[0]
    tn = 1024
    return pl.pallas_call(
        _adaln_body,
        grid=(nl, n // tn),
        in_specs=[
            pl.BlockSpec((bsz, d), lambda l, j: (0, 0)),
            pl.BlockSpec((None, d, tn), lambda l, j: (l, 0, j)),
            pl.BlockSpec((None, 1, tn), lambda l, j: (l, 0, j)),
        ],
        out_specs=pl.BlockSpec((None, bsz, tn), lambda l, j: (l, 0, j)),
        out_shape=jax.ShapeDtypeStruct((nl, bsz, n), F32),
        compiler_params=_params("arbitrary", "arbitrary"),
        name="adaln",
    )(c, w, b.reshape(nl, 1, n))


def _ssm_in_body(x_ref, sh_ref, sc_ref, w_ref, z_ref, xbc_ref, dt_ref, *, d_inner, conv_dim, heads):
    h = (x_ref[...] * (1.0 + sc_ref[...]) + sh_ref[...]).astype(BF16)
    z_ref[...] = _dot(h, w_ref[:, 0:d_inner]).astype(BF16)
    xbc_ref[...] = _dot(h, w_ref[:, d_inner:d_inner + conv_dim]).astype(BF16)
    dt_ref[...] = _dot(h, w_ref[:, d_inner + conv_dim:])[:, :heads]


def _ssm_in(x, shift, scale, w_in, *, d_inner, conv_dim, heads):
    bsz, seq, d = x.shape
    tm = min(PROJ_ROWS, seq)
    n_in = d_inner + conv_dim + heads
    n_pad = -n_in % LANES
    w = jnp.pad(w_in.astype(BF16), ((0, 0), (0, n_pad)))
    vec = pl.BlockSpec((None, 1, d), lambda b, i: (b, 0, 0))
    return pl.pallas_call(
        functools.partial(_ssm_in_body, d_inner=d_inner, conv_dim=conv_dim, heads=heads),
        grid=(bsz, seq // tm),
        in_specs=[
            pl.BlockSpec((None, tm, d), lambda b, i: (b, i, 0)),
            vec, vec,
            pl.BlockSpec((d, n_in + n_pad), lambda b, i: (0, 0)),
        ],
        out_specs=[
            pl.BlockSpec((None, tm, d_inner), lambda b, i: (b, i, 0)),
            pl.BlockSpec((None, tm, conv_dim), lambda b, i: (b, i, 0)),
            pl.BlockSpec((None, tm, heads), lambda b, i: (b, i, 0)),
        ],
        out_shape=[
            jax.ShapeDtypeStruct((bsz, seq, d_inner), BF16),
            jax.ShapeDtypeStruct((bsz, seq, conv_dim), BF16),
            jax.ShapeDtypeStruct((bsz, seq, heads), F32),
        ],
        compiler_params=_params("arbitrary", "arbitrary"),
        name="ssm_in",
    )(x, shift, scale, w)


def _ssm_core_body(z_ref, xbc_ref, dt_ref, x_ref, g1_ref, cw_ref, cb_ref, dtb_ref, alog_ref, dskip_ref,
                   nw_ref, wout_ref, lng_ref, lnb_ref, o_ref, tail_s, state, ybuf, *, heads, d_inner):
    q = SSD_CHUNK
    p_dim, n_dim = SSM_HEAD_DIM, SSM_STATE
    gn = SSM_GROUPS * n_dim
    hpg = heads // SSM_GROUPS
    tail = BF16_TILE_ROWS

    @pl.when(pl.program_id(1) == 0)
    def _():
        tail_s[...] = jnp.zeros(tail_s.shape, BF16)
        state[...] = jnp.zeros(state.shape, F32)

    u_ext = jnp.concatenate([tail_s[...], xbc_ref[...]], axis=0)
    tail_s[...] = xbc_ref[q - tail:q, :]
    tr = lax.broadcasted_iota(jnp.int32, (q, tail + q), 0)
    tc = lax.broadcasted_iota(jnp.int32, (q, tail + q), 1)
    shifts = [jnp.where(tc == tr + (tail - (SSM_CONV - 1) + k), 1.0, 0.0).astype(BF16) for k in range(SSM_CONV - 1)]
    chunks = []
    for c0 in range(0, u_ext.shape[1], CONV_COLS):
        cs = slice(c0, c0 + CONV_COLS)
        acc = cb_ref[:, cs] + cw_ref[SSM_CONV - 1:SSM_CONV, cs] * xbc_ref[:, cs].astype(F32)
        for k in range(SSM_CONV - 1):
            acc = acc + cw_ref[k:k + 1, cs] * _dot(shifts[k], u_ext[:, cs])
        chunks.append(_silu(acc))
    act = jnp.concatenate(chunks, axis=1)

    dt = dt_ref[...] + dtb_ref[...]
    dt = jnp.maximum(dt, 0.0) + jnp.log1p(jnp.exp(-jnp.abs(dt)))
    d_a = dt * (-LOG2E * jnp.exp(alog_ref[...]))
    row = lax.broadcasted_iota(jnp.int32, (q, q), 0)
    col = lax.broadcasted_iota(jnp.int32, (q, q), 1)
    causal = row >= col
    acum = jnp.dot(causal.astype(F32), d_a, precision=HIGHEST, preferred_element_type=F32)
    acum_t = acum.T
    dt_t = dt.T
    er = lax.broadcasted_iota(jnp.int32, (heads, d_inner), 0)
    ec = lax.broadcasted_iota(jnp.int32, (heads, d_inner), 1) // p_dim
    expand = jnp.where(er == ec, 1.0, 0.0).astype(BF16)

    def per_channel(v):
        hi = v.astype(BF16)
        lo = (v - hi.astype(F32)).astype(BF16)
        return _dot(hi, expand) + _dot(lo, expand)

    e_acum_x = per_channel(jnp.exp2(acum))
    w_end_x = per_channel(jnp.exp2(acum[q - 1:q, :] - acum) * dt)
    e_last_x = e_acum_x[q - 1:q, :]
    gw = hpg * p_dim
    lane = lax.broadcasted_iota(jnp.int32, (q, LANES), 1)

    for g in range(SSM_GROUPS):
        gs = slice(g * gw, (g + 1) * gw)
        b_g = act[:, d_inner + g * n_dim:d_inner + (g + 1) * n_dim].astype(BF16)
        c_g = act[:, d_inner + gn + g * n_dim:d_inner + gn + (g + 1) * n_dim].astype(BF16)
        cb = _dot_nt(c_g, b_g)
        xs_g = act[:, gs]
        st = state[g]
        y_off = _dot(c_g, st.astype(BF16)) * e_acum_x[:, gs]
        xw = (xs_g * w_end_x[:, gs]).astype(BF16)
        state[g] = st * e_last_x[:, gs] + _dot_tn(b_g, xw)
        for pr in range(gw // LANES):
            ms = []
            for h in range(g * hpg + 2 * pr, g * hpg + 2 * pr + 2):
                seg = acum[:, h:h + 1] - acum_t[h:h + 1, :]
                ms.append(cb * jnp.exp2(jnp.where(causal, seg, -jnp.inf)) * dt_t[h:h + 1, :])
            pair = xs_g[:, pr * LANES:(pr + 1) * LANES]
            rhs = jnp.concatenate([jnp.where(lane < p_dim, pair, 0.0), jnp.where(lane >= p_dim, pair, 0.0)], axis=0)
            y = _dot(jnp.concatenate(ms, axis=1).astype(BF16), rhs.astype(BF16))
            cs = slice(g * gw + pr * LANES, g * gw + (pr + 1) * LANES)
            ybuf[:, cs] = y + y_off[:, pr * LANES:(pr + 1) * LANES] + dskip_ref[:, cs] * pair

    y = ybuf[...] * _silu(z_ref[...].astype(F32))
    y = y * lax.rsqrt(jnp.mean(y * y, axis=-1, keepdims=True) + RMS_EPS) * nw_ref[...]
    out = _dot(y.astype(BF16), wout_ref[...])
    r_sum = DN_ALPHA * x_ref[...] + (1.0 + g1_ref[...]) * out
    o_ref[...] = _layer_norm(r_sum, lng_ref[...], lnb_ref[...])


def _ssm_core(z, xbc, dt_raw, x, gate1, conv_w, conv_b, dt_bias, a_log, d_skip, norm_w, w_out, ln_g, ln_b):
    bsz, seq, d = x.shape
    d_inner = z.shape[-1]
    conv_dim = xbc.shape[-1]
    heads = dt_raw.shape[-1]
    q = SSD_CHUNK
    rows = lambda n: pl.BlockSpec((None, q, n), lambda b, i: (b, i, 0))
    full = lambda a: pl.BlockSpec(a.shape, lambda b, i: (0,) * a.ndim)
    conv_b, dt_bias, a_log = conv_b[None], dt_bias[None], a_log[None]
    norm_w, ln_g, ln_b = norm_w[None], ln_g[None], ln_b[None]
    w_out = w_out.astype(BF16)
    d_skip = jnp.repeat(d_skip, SSM_HEAD_DIM)[None]
    return pl.pallas_call(
        functools.partial(_ssm_core_body, heads=heads, d_inner=d_inner),
        grid=(bsz, seq // q),
        in_specs=[
            rows(d_inner), rows(conv_dim), rows(heads), rows(d),
            pl.BlockSpec((None, 1, d), lambda b, i: (b, 0, 0)),
            full(conv_w), full(conv_b), full(dt_bias), full(a_log), full(d_skip),
            full(norm_w), full(w_out), full(ln_g), full(ln_b),
        ],
        out_specs=rows(d),
        out_shape=jax.ShapeDtypeStruct((bsz, seq, d), F32),
        scratch_shapes=[
            pltpu.VMEM((BF16_TILE_ROWS, conv_dim), BF16),
            pltpu.VMEM((SSM_GROUPS, SSM_STATE, d_inner // SSM_GROUPS), F32),
            pltpu.VMEM((q, d_inner), F32),
        ],
        compiler_params=_params("arbitrary", "arbitrary"),
        name="ssm_core",
    )(z, xbc, dt_raw, x, gate1, conv_w, conv_b, dt_bias, a_log, d_skip, norm_w, w_out, ln_g, ln_b)


AUG_LANES = LANES


def _split3(v):
    hi = v.astype(BF16)
    r = v - hi.astype(F32)
    mid = r.astype(BF16)
    lo = (r - mid.astype(F32)).astype(BF16)
    return hi, mid, lo


PART_STRIDE = LANES // 16


def _place(parts, sign):
    heads = parts[0].shape[1]
    r = lax.broadcasted_iota(jnp.int32, (heads, LANES), 0)
    c = lax.broadcasted_iota(jnp.int32, (heads, LANES), 1)
    out = None
    for k, part in enumerate(parts):
        t = _dot(part, jnp.where(c == r * PART_STRIDE + k, sign, 0.0).astype(BF16))
        out = t if out is None else out + t
    return out


def _head_parts(compact, h, first_lane, lane):
    rolled = pltpu.roll(compact, (first_lane - h * PART_STRIDE) % LANES, axis=1)
    return jnp.where((lane >= first_lane) & (lane < first_lane + 3), rolled, 0.0)


def _head_window(m, h, hd):
    base = (h * hd // LANES) * LANES
    w = m[:, base:base + LANES]
    shift = (h * hd) % LANES
    return pltpu.roll(w, LANES - shift, axis=1) if shift else w


def _kv_body(x_ref, sh_ref, sc_ref, w_ref, fb_ref, k_ref, v_ref, cfp_ref, carry, *, d, heads):
    hd = ATTN_HEAD_DIM

    @pl.when(pl.program_id(1) == 0)
    def _():
        carry[...] = jnp.zeros(carry.shape, F32)

    u = (x_ref[...] * (1.0 + sc_ref[...]) + sh_ref[...]).astype(BF16)
    kf = _dot(u, w_ref[:, 0:d])
    vf = _dot(u, w_ref[:, d:2 * d])
    f = _dot(u, w_ref[:, 2 * d:])[:, :heads] + fb_ref[...]
    log_f = jnp.minimum(f, 0.0) - jnp.log1p(jnp.exp(-jnp.abs(f)))
    lt = log_f.T
    tm = lt.shape[1]
    lane_t = lax.broadcasted_iota(jnp.int32, lt.shape, 1)
    step = 1
    while step < tm:
        lt = lt + jnp.where(lane_t >= step, pltpu.roll(lt, step, axis=1), 0.0)
        step *= 2
    lt = lt + carry[...]
    carry[...] = lt[:, tm - 1:tm]
    parts = _split3(lt.T * LOG2E)
    for n in range(3):
        cfp_ref[n] = parts[n]
    compact = _place(parts, -1.0)
    lane = lax.broadcasted_iota(jnp.int32, (tm, AUG_LANES), 1)
    ones_k = jnp.where((lane >= hd) & (lane < hd + 3), 1.0, 0.0)
    ones_v = jnp.where(lane == hd, 1.0, 0.0)
    for h in range(heads):
        ext = _head_parts(compact, h, hd + 3, lane) + ones_k
        k_ref[h] = jnp.where(lane < hd, _head_window(kf, h, hd), ext).astype(BF16)
        v_ref[h] = jnp.where(lane < hd, _head_window(vf, h, hd), ones_v).astype(BF16)


def _shared_kv(x, shift, scale, kv_w, kv_fb):
    bsz, seq, d = x.shape
    heads = kv_w.shape[1] - 2 * d
    tm = min(PROJ_ROWS, seq)
    n_pad = -kv_w.shape[1] % LANES
    w = jnp.pad(kv_w.astype(BF16), ((0, 0), (0, n_pad)))
    vec = pl.BlockSpec((None, 1, d), lambda b, i: (b, 0, 0))
    aug = pl.BlockSpec((None, heads, tm, AUG_LANES), lambda b, i: (b, 0, i, 0))
    return pl.pallas_call(
        functools.partial(_kv_body, d=d, heads=heads),
        grid=(bsz, seq // tm),
        in_specs=[pl.BlockSpec((None, tm, d), lambda b, i: (b, i, 0)), vec, vec,
                  pl.BlockSpec(w.shape, lambda b, i: (0, 0)),
                  pl.BlockSpec((1, heads), lambda b, i: (0, 0))],
        out_specs=[aug, aug, pl.BlockSpec((None, 3, tm, heads), lambda b, i: (b, 0, i, 0))],
        out_shape=[
            jax.ShapeDtypeStruct((bsz, heads, seq, AUG_LANES), BF16),
            jax.ShapeDtypeStruct((bsz, heads, seq, AUG_LANES), BF16),
            jax.ShapeDtypeStruct((bsz, 3, seq, heads), BF16),
        ],
        scratch_shapes=[pltpu.VMEM((heads, 1), F32)],
        compiler_params=_params("arbitrary", "arbitrary"),
        name="shared_kv",
    )(x, shift, scale, w, kv_fb[None])


def _attn_block(q_s, k_ref, v_ref, m_s, acc_s, *, heads, t, diagonal):
    def head(h, carry):
        s = _dot_nt(q_s[h], k_ref[h])
        if diagonal:
            qi = lax.broadcasted_iota(jnp.int32, (t, t), 0)
            ki = lax.broadcasted_iota(jnp.int32, (t, t), 1)
            s = jnp.where(qi >= ki, s, -jnp.inf)
        m_prev = m_s[h]
        m_new = jnp.maximum(m_prev, jnp.max(s, axis=1, keepdims=True))
        p = jnp.concatenate(
            [jnp.exp2(s[:, c * LANES:(c + 1) * LANES] - m_new) for c in range(t // LANES)], axis=1)
        acc_s[h] = jnp.exp2(m_prev - m_new) * acc_s[h] + _dot(p.astype(BF16), v_ref[h])
        m_s[h] = m_new
        return carry

    lax.fori_loop(0, heads, head, 0, unroll=ATTN_HEAD_UNROLL)


def _attn_body(qi_ref, kj_ref, x_ref, sh_ref, sc_ref, g1_ref, wq_ref, wo_ref, k_ref, v_ref, cfp_ref, lng_ref, lnb_ref,
               o_ref, q_s, m_s, acc_s, o_s, *, heads, t):
    i = qi_ref[pl.program_id(1)]
    j = kj_ref[pl.program_id(1)]
    hd = ATTN_HEAD_DIM

    @pl.when(j == 0)
    def _():
        h = (x_ref[...] * (1.0 + sc_ref[...]) + sh_ref[...]).astype(BF16)
        qf = _dot(h, wq_ref[...]) * (hd ** -0.5 * LOG2E)
        compact = _place([cfp_ref[n] for n in range(3)], 1.0)
        lane = lax.broadcasted_iota(jnp.int32, (t, AUG_LANES), 1)
        ones_q = jnp.where((lane >= hd + 3) & (lane < hd + 6), 1.0, 0.0)
        for hh in range(heads):
            ext = _head_parts(compact, hh, hd, lane) + ones_q
            q_s[hh] = jnp.where(lane < hd, _head_window(qf, hh, hd), ext).astype(BF16)
        m_s[...] = jnp.full(m_s.shape, -jnp.inf, F32)
        acc_s[...] = jnp.zeros(acc_s.shape, F32)

    block = functools.partial(_attn_block, q_s, k_ref, v_ref, m_s, acc_s, heads=heads, t=t)
    pl.when(j < i)(functools.partial(block, diagonal=False))
    pl.when(j == i)(functools.partial(block, diagonal=True))

    @pl.when(j == i)
    def _():
        for hh in range(heads):
            a = acc_s[hh]
            o_s[:, hh * hd:(hh + 1) * hd] = (a / a[:, hd:hd + 1])[:, :hd].astype(BF16)
        y = _dot(o_s[...], wo_ref[...])
        r_sum = DN_ALPHA * x_ref[...] + (1.0 + g1_ref[...]) * y
        o_ref[...] = _layer_norm(r_sum, lng_ref[...], lnb_ref[...])


def _attention(x, shift, scale, gate1, w_q, w_o, k_aug, v_aug, cf_parts, ln_g, ln_b):
    bsz, seq, d = x.shape
    heads = k_aug.shape[1]
    t = min(ATTN_ROWS, seq)
    pairs = [(i, j) for i in range(seq // t) for j in range(i + 1)]
    qi = jnp.asarray([p[0] for p in pairs], jnp.int32)
    kj = jnp.asarray([p[1] for p in pairs], jnp.int32)
    vec = pl.BlockSpec((None, 1, d), lambda b, p, qi, kj: (b, 0, 0))
    full = lambda a: pl.BlockSpec(a.shape, lambda b, p, qi, kj: (0,) * a.ndim)
    kv_spec = pl.BlockSpec((None, heads, t, AUG_LANES), lambda b, p, qi, kj: (b, 0, kj[p], 0))
    w_q, w_o, ln_g, ln_b = w_q.astype(BF16), w_o.astype(BF16), ln_g[None], ln_b[None]
    return pl.pallas_call(
        functools.partial(_attn_body, heads=heads, t=t),
        grid_spec=pltpu.PrefetchScalarGridSpec(
            num_scalar_prefetch=2, grid=(bsz, len(pairs)),
            in_specs=[
                pl.BlockSpec((None, t, d), lambda b, p, qi, kj: (b, qi[p], 0)),
                vec, vec, vec, full(w_q), full(w_o), kv_spec, kv_spec,
                pl.BlockSpec((None, 3, t, heads), lambda b, p, qi, kj: (b, 0, qi[p], 0)),
                full(ln_g), full(ln_b),
            ],
            out_specs=pl.BlockSpec((None, t, d), lambda b, p, qi, kj: (b, qi[p], 0)),
            scratch_shapes=[
                pltpu.VMEM((heads, t, AUG_LANES), BF16),
                pltpu.VMEM((heads, t, LANES), F32),
                pltpu.VMEM((heads, t, AUG_LANES), F32),
                pltpu.VMEM((t, d), BF16),
            ]),
        out_shape=jax.ShapeDtypeStruct((bsz, seq, d), F32),
        compiler_params=_params("arbitrary", "arbitrary"),
        name="fox_attention",
    )(qi, kj, x, shift, scale, gate1, w_q, w_o, k_aug, v_aug, cf_parts, ln_g, ln_b)


def _route(sel, scores):
    n_e, t = sel.shape
    per = n_e // N_EXPERT_GROUPS
    sub = lax.broadcasted_iota(jnp.int32, (per, t), 0)
    neg = -jnp.inf
    gs = jnp.zeros((N_EXPERT_GROUPS, t), F32)
    gidx = lax.broadcasted_iota(jnp.int32, (N_EXPERT_GROUPS, t), 0)
    for g in range(N_EXPERT_GROUPS):
        v = sel[g * per:(g + 1) * per, :]
        m1 = jnp.max(v, axis=0, keepdims=True)
        first = jnp.min(jnp.where(v == m1, sub, per), axis=0, keepdims=True)
        m2 = jnp.max(jnp.where(sub == first, neg, v), axis=0, keepdims=True)
        gs = jnp.where(gidx == g, m1 + m2, gs)
    grank = jnp.zeros((N_EXPERT_GROUPS, t), jnp.int32)
    for g in range(N_EXPERT_GROUPS):
        other = gs[g:g + 1, :]
        beats = (other > gs) | ((other >= gs) & (gidx > g))
        grank = grank + jnp.where(beats, 1, 0)
    masked = jnp.concatenate(
        [jnp.where(grank[g:g + 1, :] < TOPK_GROUPS, sel[g * per:(g + 1) * per, :], neg)
         for g in range(N_EXPERT_GROUPS)], axis=0)
    eidx = lax.broadcasted_iota(jnp.int32, (n_e, t), 0)
    work = masked
    w = jnp.zeros((n_e, t), F32)
    chosen = jnp.zeros((n_e, t), F32)
    for _ in range(TOP_K):
        top = jnp.max(work, axis=0, keepdims=True)
        first = jnp.min(jnp.where(work == top, eidx, n_e), axis=0, keepdims=True)
        pick = eidx == first
        w = jnp.where(pick, scores, w)
        chosen = jnp.where(pick, 1.0, chosen)
        work = jnp.where(pick, neg, work)
    return w / jnp.sum(w, axis=0, keepdims=True) * ROUTED_SCALE, chosen


def _pack_halves(v):
    half = v.shape[1] // 2
    bits = lambda a: lax.bitcast_convert_type(a.astype(BF16).astype(F32), jnp.uint32)
    word = (bits(v[:, half:]) & jnp.uint32(0xFFFF0000)) | (bits(v[:, :half]) >> 16)
    return lax.bitcast_convert_type(word, jnp.int32)


def _unpack_halves(w):
    u = lax.bitcast_convert_type(w, jnp.uint32)
    lo = lax.bitcast_convert_type(u << 16, F32)
    hi = lax.bitcast_convert_type(u & jnp.uint32(0xFFFF0000), F32)
    return lo, hi


def _dot_halves(lo, hi, w_ref):
    half = lo.shape[1]
    return _dot(lo.astype(BF16), w_ref[:half, :]) + _dot(hi.astype(BF16), w_ref[half:, :])


def _route_body(x_ref, sh_ref, sc_ref, wr_ref, rb_ref, hp_ref, e8_ref, p8_ref, g8_ref, cnt_ref, carry, upper):
    @pl.when((pl.program_id(0) == 0) & (pl.program_id(1) == 0))
    def _():
        carry[...] = jnp.zeros(carry.shape, F32)
        ur = lax.broadcasted_iota(jnp.int32, upper.shape, 0)
        uc = lax.broadcasted_iota(jnp.int32, upper.shape, 1)
        upper[...] = jnp.where(ur <= uc, 1.0, 0.0).astype(BF16)

    h = x_ref[...] * (1.0 + sc_ref[...]) + sh_ref[...]
    hp_ref[...] = _pack_halves(h)
    scores = _sigmoid(_dot_nt(wr_ref[...], h, precision=HIGHEST))
    gate, chosen = _route(scores + rb_ref[...], scores)
    n_e, tm = chosen.shape
    incl = _dot(chosen.astype(BF16), upper[...])
    pos = carry[...] + incl - chosen
    carry[...] = carry[...] + incl[:, tm - 1:tm]
    cnt_ref[...] = jnp.broadcast_to(carry[...], cnt_ref.shape)
    er = lax.broadcasted_iota(jnp.int32, (n_e, n_e), 0)
    ec = lax.broadcasted_iota(jnp.int32, (n_e, n_e), 1)
    before = jnp.where(ec < er, 1.0, 0.0).astype(BF16)
    rank = _dot(before, chosen.astype(BF16))
    eidx = lax.broadcasted_iota(jnp.int32, (n_e, tm), 0).astype(F32)
    row = lax.broadcasted_iota(jnp.int32, (TOP_K, tm), 0)
    e8 = jnp.zeros((TOP_K, tm), F32)
    p8 = jnp.zeros((TOP_K, tm), F32)
    g8 = jnp.zeros((TOP_K, tm), F32)
    for k in range(TOP_K):
        sel = jnp.where((rank == k) & (chosen > 0.0), 1.0, 0.0)
        e8 = jnp.where(row == k, jnp.sum(sel * eidx, axis=0, keepdims=True), e8)
        p8 = jnp.where(row == k, jnp.sum(sel * pos, axis=0, keepdims=True), p8)
        g8 = jnp.where(row == k, jnp.sum(sel * gate, axis=0, keepdims=True), g8)
    e8_ref[...] = e8.astype(jnp.int32)
    p8_ref[...] = p8.astype(jnp.int32)
    g8_ref[...] = g8


def _moe_route(x, shift, scale, w_router, router_bias):
    bsz, seq, d = x.shape
    n_e = w_router.shape[1]
    tm = min(MOE_ROWS, seq)
    nt = seq // tm
    tokens = bsz * seq
    vec = pl.BlockSpec((None, 1, d), lambda b, i: (b, 0, 0))
    per_tok = pl.BlockSpec((TOP_K, tm), lambda b, i: (0, b * nt + i))
    wr_t = w_router.T
    rb = router_bias[:, None]
    return pl.pallas_call(
        _route_body,
        grid=(bsz, nt),
        in_specs=[pl.BlockSpec((None, tm, d), lambda b, i: (b, i, 0)), vec, vec,
                  pl.BlockSpec(wr_t.shape, lambda b, i: (0, 0)), pl.BlockSpec(rb.shape, lambda b, i: (0, 0))],
        out_specs=[pl.BlockSpec((tm, d // 2), lambda b, i: (b * nt + i, 0)), per_tok, per_tok, per_tok,
                   pl.BlockSpec((n_e, LANES), lambda b, i: (0, 0))],
        out_shape=[
            jax.ShapeDtypeStruct((tokens, d // 2), jnp.int32),
            jax.ShapeDtypeStruct((TOP_K, tokens), jnp.int32),
            jax.ShapeDtypeStruct((TOP_K, tokens), jnp.int32),
            jax.ShapeDtypeStruct((TOP_K, tokens), F32),
            jax.ShapeDtypeStruct((n_e, LANES), F32),
        ],
        scratch_shapes=[pltpu.VMEM((n_e, 1), F32), pltpu.VMEM((tm, tm), BF16)],
        compiler_params=_params("arbitrary", "arbitrary"),
        name="moe_route",
    )(x, shift, scale, wr_t, rb)


def _dest_body(base_ref, e8_ref, p8_ref, o_ref, *, n_e):
    e8 = e8_ref[...]
    dest = p8_ref[...]
    for e in range(n_e):
        dest = dest + jnp.where(e8 == e, base_ref[e], 0)
    o_ref[...] = dest


def _moe_dest(base, e8, p8):
    tokens = e8.shape[1]
    tb = min(2048, tokens)
    blk = pl.BlockSpec((TOP_K, tb), lambda i, base: (0, i))
    return pl.pallas_call(
        functools.partial(_dest_body, n_e=base.shape[0]),
        grid_spec=pltpu.PrefetchScalarGridSpec(num_scalar_prefetch=1, grid=(tokens // tb,), in_specs=[blk, blk],
                                               out_specs=blk),
        out_shape=jax.ShapeDtypeStruct((TOP_K, tokens), jnp.int32),
        compiler_params=_params("arbitrary"),
        name="moe_dest",
    )(base, e8, p8)


def _sc_mesh():
    return plsc.VectorSubcoreMesh(core_axis_name="c", subcore_axis_name="s")


def _sc_dispatch(rows, dest, n_slots):
    tokens, width = rows.shape
    chunk = dest.shape[2]
    sc = plsc.get_sparse_core_info()
    n_cores, n_workers = sc.num_cores, sc.num_cores * sc.num_subcores
    t_per_w = tokens // n_workers
    n_chunks = t_per_w // chunk
    assert n_chunks * chunk * n_workers == tokens and n_chunks % 2 == 0

    def body(rows_hbm, dest_hbm, out_hbm, idx0, idx1, buf0, buf1, rs0, rs1, ws0, ws1):
        idx, bufs, rsem, wsem = (idx0, idx1), (buf0, buf1), (rs0, rs1), (ws0, ws1)
        wid = lax.axis_index("s") * n_cores + lax.axis_index("c")

        def read(i, b):
            return pltpu.make_async_copy(rows_hbm.at[pl.ds(wid * t_per_w + i * chunk, chunk)], bufs[b], rsem[b])

        def scatter(b, k):
            return pltpu.make_async_copy(bufs[b], out_hbm.at[idx[b].at[k]], wsem[b])

        read(0, 0).start()

        def pair(g, carry):
            for b in range(2):
                i = g * 2 + b

                @pl.when(i + 1 < n_chunks)
                def _():
                    @pl.when(i >= 1)
                    def _():
                        for k in range(TOP_K):
                            scatter(1 - b, k).wait()
                    read(i + 1, 1 - b).start()

                pltpu.sync_copy(dest_hbm.at[wid * n_chunks + i], idx[b])
                read(i, b).wait()
                for k in range(TOP_K):
                    scatter(b, k).start()
            return carry

        lax.fori_loop(0, n_chunks // 2, pair, 0)
        for b in range(2):
            for k in range(TOP_K):
                scatter(b, k).wait()

    return pl.kernel(
        body, mesh=_sc_mesh(), out_type=jax.ShapeDtypeStruct((n_slots, width), jnp.int32),
        scratch_types=[pltpu.VMEM((TOP_K, chunk), jnp.int32)] * 2 + [pltpu.VMEM((chunk, width), jnp.int32)] * 2
        + [pltpu.SemaphoreType.DMA] * 4,
    )(rows, dest)


def _sc_return(ys, dest):
    width = ys.shape[1]
    n_all, _, chunk = dest.shape
    tokens = n_all * chunk
    sc = plsc.get_sparse_core_info()
    n_cores, n_workers = sc.num_cores, sc.num_cores * sc.num_subcores
    t_per_w = tokens // n_workers
    n_chunks = t_per_w // chunk
    n_items = n_chunks * TOP_K
    assert n_chunks * chunk * n_workers == tokens

    def body(ys_hbm, dest_hbm, out_hbm, idx, buf0, buf1, gs0, gs1, ws0, ws1):
        bufs, gsem, wsem = (buf0, buf1), (gs0, gs1), (ws0, ws1)
        wid = lax.axis_index("s") * n_cores + lax.axis_index("c")
        pltpu.sync_copy(dest_hbm.at[pl.ds(wid * n_chunks, n_chunks)], idx)

        def gather(it, b):
            return pltpu.make_async_copy(ys_hbm.at[idx.at[it // TOP_K, it % TOP_K]], bufs[b], gsem[b])

        def write(it, b):
            dst = out_hbm.at[it % TOP_K, pl.ds(wid * t_per_w + (it // TOP_K) * chunk, chunk)]
            return pltpu.make_async_copy(bufs[b], dst, wsem[b])

        gather(0, 0).start()

        def pair(g, carry):
            for b in range(2):
                it = g * 2 + b

                @pl.when(it + 1 < n_items)
                def _():
                    @pl.when(it >= 1)
                    def _():
                        write(it - 1, 1 - b).wait()
                    gather(it + 1, 1 - b).start()

                gather(it, b).wait()
                write(it, b).start()
            return carry

        lax.fori_loop(0, n_items // 2, pair, 0)
        write(n_items - 2, 0).wait()
        write(n_items - 1, 1).wait()

    return pl.kernel(
        body, mesh=_sc_mesh(), out_type=jax.ShapeDtypeStruct((TOP_K, tokens, width), jnp.int32),
        scratch_types=[pltpu.VMEM((n_chunks, TOP_K, chunk), jnp.int32)] + [pltpu.VMEM((chunk, width), jnp.int32)] * 2
        + [pltpu.SemaphoreType.DMA] * 4,
    )(ys, dest)


def _experts_body(te_ref, nu_ref, xs_ref, *refs, tile, group):
    w13_refs, w2_refs, ys_ref = refs[:group], refs[group:2 * group], refs[2 * group]
    @pl.when(pl.program_id(0) * group < nu_ref[0])
    def _():
        for j in range(group):
            rows = pl.ds(j * tile, tile)
            lo, hi = _unpack_halves(xs_ref[rows, :])
            h = _dot_halves(lo, hi, w13_refs[j])
            f = h.shape[1] // 2
            a = _silu(h[:, :f]) * h[:, f:]
            ys_ref[rows, :] = _pack_halves(_dot(a.astype(BF16), w2_refs[j][...]))


def _moe_experts(xs, tile_expert, n_used, w13, w2):
    n_slots, half = xs.shape
    tile, group = MOE_SLOT_TILE, MOE_TILES_PER_STEP
    n_e, d, f2 = w13.shape
    owner = lambda j: (lambda i, te, nu: (te[i * group + j], 0, 0))
    rows = pl.BlockSpec((group * tile, half), lambda i, te, nu: (i, 0))
    return pl.pallas_call(
        functools.partial(_experts_body, tile=tile, group=group),
        grid_spec=pltpu.PrefetchScalarGridSpec(
            num_scalar_prefetch=2, grid=(n_slots // (group * tile),),
            in_specs=[rows] + [pl.BlockSpec((None, d, f2), owner(j)) for j in range(group)]
            + [pl.BlockSpec((None, f2 // 2, d), owner(j)) for j in range(group)],
            out_specs=rows),
        out_shape=jax.ShapeDtypeStruct((n_slots, half), jnp.int32),
        compiler_params=_params("arbitrary"),
        name="moe_experts",
    )(tile_expert, n_used, xs, *([w13] * group), *([w2] * group))


def _combine_body(x_ref, hp_ref, yk_ref, g8_ref, g2_ref, ws1_ref, ws3_ref, ws2_ref, lng_ref, lnb_ref, o_ref):
    lo, hi = _unpack_halves(hp_ref[...])
    a = _silu(_dot_halves(lo, hi, ws1_ref)) * _dot_halves(lo, hi, ws3_ref)
    shared = _dot(a.astype(BF16), ws2_ref[...])
    half = lo.shape[1]
    acc_lo, acc_hi = shared[:, :half], shared[:, half:]
    gt = g8_ref[...].T
    for k in range(TOP_K):
        y_lo, y_hi = _unpack_halves(yk_ref[k])
        acc_lo = acc_lo + gt[:, k:k + 1] * y_lo
        acc_hi = acc_hi + gt[:, k:k + 1] * y_hi
    y = jnp.concatenate([acc_lo, acc_hi], axis=1)
    r_sum = DN_ALPHA * x_ref[...] + (1.0 + g2_ref[...]) * y
    o_ref[...] = _layer_norm(r_sum, lng_ref[...], lnb_ref[...])


def _moe_combine(x, hp, yk, g8, gate2, ws1, ws3, ws2, ln_g, ln_b):
    bsz, seq, d = x.shape
    tm = min(MOE_COMBINE_ROWS, seq)
    nt = seq // tm
    ws1, ws3, ws2 = ws1.astype(BF16), ws3.astype(BF16), ws2.astype(BF16)
    ln_g, ln_b = ln_g[None], ln_b[None]
    full = lambda a: pl.BlockSpec(a.shape, lambda b, i: (0,) * a.ndim)
    return pl.pallas_call(
        _combine_body,
        grid=(bsz, nt),
        in_specs=[
            pl.BlockSpec((None, tm, d), lambda b, i: (b, i, 0)),
            pl.BlockSpec((tm, d // 2), lambda b, i: (b * nt + i, 0)),
            pl.BlockSpec((TOP_K, tm, d // 2), lambda b, i: (0, b * nt + i, 0)),
            pl.BlockSpec((TOP_K, tm), lambda b, i: (0, b * nt + i)),
            pl.BlockSpec((None, 1, d), lambda b, i: (b, 0, 0)),
            full(ws1), full(ws3), full(ws2), full(ln_g), full(ln_b),
        ],
        out_specs=pl.BlockSpec((None, tm, d), lambda b, i: (b, i, 0)),
        out_shape=jax.ShapeDtypeStruct((bsz, seq, d), F32),
        compiler_params=_params("arbitrary", "arbitrary"),
        name="moe_combine",
    )(x, hp, yk, g8, gate2, ws1, ws3, ws2, ln_g, ln_b)


def _moe(x, shift, scale, gate2, w_router, router_bias, w13, w2, ws1, ws3, ws2, ln_g, ln_b):
    bsz, seq, d = x.shape
    tokens = bsz * seq
    n_e = w13.shape[0]
    tile = MOE_SLOT_TILE
    n_slots = tokens * TOP_K + n_e * tile
    hp, e8, p8, g8, counts = _moe_route(x, shift, scale, w_router, router_bias)
    cnt = counts[:, 0].astype(jnp.int32)
    padded = (cnt + (tile - 1)) // tile * tile
    ends = jnp.cumsum(padded)
    base = ends - padded
    tile_start = jnp.arange(n_slots // tile, dtype=jnp.int32) * tile
    tile_expert = jnp.minimum(jnp.sum((tile_start[:, None] >= ends[None, :]).astype(jnp.int32), axis=1), n_e - 1)
    n_used = (ends[-1:] // tile).astype(jnp.int32)
    dest8 = _moe_dest(base, e8, p8)
    chunk = MOE_SC_CHUNK
    dest = dest8.reshape(TOP_K, tokens // chunk, chunk).transpose(1, 0, 2)
    xs = _sc_dispatch(hp, dest, n_slots)
    ys = _moe_experts(xs, tile_expert, n_used, w13, w2)
    yk = _sc_return(ys, dest)
    return _moe_combine(x, hp, yk, g8, gate2, ws1, ws3, ws2, ln_g, ln_b)


def kernel(x, c, ada_w, ada_b, ln1_g, ln1_b, ln2_g, ln2_b, ssm_w_in, ssm_conv_w, ssm_conv_b, ssm_dt_bias,
           ssm_a_log, ssm_d, ssm_norm_w, ssm_w_out, kv_ada_w, kv_ada_b, kv_w, kv_fb, attn_w_q, attn_w_o,
           moe_w_router, moe_bias, moe_w1, moe_w3, moe_w2, moe_ws1, moe_ws3, moe_ws2):
    d = x.shape[-1]
    mods = _adaln(c, ada_w, ada_b)
    kv_mod = _adaln(c, kv_ada_w[None], kv_ada_b[None])[0]
    heads = ssm_dt_bias.shape[-1]
    d_inner = ssm_norm_w.shape[-1]
    conv_dim = ssm_conv_w.shape[-1]
    w13 = jnp.concatenate([moe_w1, moe_w3], axis=-1).astype(BF16)
    w2 = moe_w2.astype(BF16)

    def layer_step(layer, x, rows, kv):
        part = lambda m, n: m[rows, None, n * d:(n + 1) * d]
        shift1, scale1, gate1, shift2, scale2, gate2 = (part(mods[layer], n) for n in range(6))
        if layer < N_A_LAYERS:
            a = layer
            z, xbc, dt_raw = _ssm_in(x, shift1, scale1, ssm_w_in[a], d_inner=d_inner, conv_dim=conv_dim, heads=heads)
            x = _ssm_core(z, xbc, dt_raw, x, gate1, ssm_conv_w[a], ssm_conv_b[a], ssm_dt_bias[a], ssm_a_log[a],
                          ssm_d[a], ssm_norm_w[a], ssm_w_out[a], ln1_g[layer], ln1_b[layer])
        else:
            b = layer - N_A_LAYERS
            x = _attention(x, shift1, scale1, gate1, attn_w_q[b], attn_w_o[b], *kv, ln1_g[layer], ln1_b[layer])
        x = _moe(x, shift2, scale2, gate2, moe_w_router[layer], moe_bias[layer], w13[layer], w2[layer],
                 moe_ws1[layer], moe_ws3[layer], moe_ws2[layer], ln2_g[layer], ln2_b[layer])
        if layer == N_A_LAYERS - 1:
            kv = _shared_kv(x, part(kv_mod, 0), part(kv_mod, 1), kv_w, kv_fb)
        return x, kv

    per = x.shape[0] // N_STREAMS
    outs = []
    for s in range(N_STREAMS):
        rows, kv = slice(s * per, (s + 1) * per), None
        xs = x[rows]
        for layer in range(DEPTH):
            xs, kv = layer_step(layer, xs, rows, kv)
        outs.append(xs)
    return jnp.concatenate(outs, axis=0)
```
